```python
import math
import jax, jax.numpy as jnp
from jax import lax
import numpy as np

D_MODEL = 1024
BATCH = 8
SEQ = 8192
DEPTH = 2
DEC_BATCH = 32
DEC_SEQ = 16
PAST_LEN = 2048

CHUNK = 64
Q_BLOCK = 128
DA_HEADS = 8
DA_HEAD_DIM = 64
DA_K_ROW = 2 * DA_HEAD_DIM
DA_V_DIM = 2 * DA_HEAD_DIM
ROPE_DIM = DA_HEAD_DIM // 4
ROPE_THETA = 500000.0
GLA_HEADS = 4
GLA_DK = D_MODEL // 2 // GLA_HEADS
GLA_DV = D_MODEL // GLA_HEADS
GLA_RANK = 16
GLA_GATE_NORM = 16.0
D_FF = 2816
CONV_W = 3
EPS = 1e-6
DA_QK = DA_HEADS * 2 * DA_HEAD_DIM
DA_V = DA_HEADS * DA_V_DIM
GLA_QK = GLA_HEADS * GLA_DK
GLA_V = GLA_HEADS * GLA_DV
D_IN = 2 * DA_QK + DA_V + 2 * GLA_QK + 2 * GLA_V + GLA_RANK + 2 * D_MODEL

kernel_name = "diffattn_gla_gated_merge_streaming_step"


def _rmsnorm(x, w):
    xf = x.astype(jnp.float32)
    y = xf * lax.rsqrt(jnp.mean(xf * xf, axis=-1, keepdims=True) + EPS)
    return (y * w.astype(jnp.float32)).astype(x.dtype)


def _rope(x, pos):
    half = ROPE_DIM // 2
    inv = ROPE_THETA ** (-jnp.arange(half, dtype=jnp.float32) * 2.0 / ROPE_DIM)
    ang = pos.astype(jnp.float32)[:, None] * inv[None, :]
    cos = jnp.cos(ang)[None, :, None, None, :].astype(x.dtype)
    sin = jnp.sin(ang)[None, :, None, None, :].astype(x.dtype)
    x1 = x[..., :half]
    x2 = x[..., half:ROPE_DIM]
    return jnp.concatenate([x1 * cos - x2 * sin, x2 * cos + x1 * sin, x[..., ROPE_DIM:]], axis=-1)


def _diff_weights(s, lam):
    p = jax.nn.softmax(s, axis=-1)
    return p[:, :, 0] - lam * p[:, :, 1]


def _diff_attn_prompt(q, k, v, lam):
    B, S = q.shape[0], q.shape[1]
    nb = S // Q_BLOCK
    qb = jnp.moveaxis(q.reshape(B, nb, Q_BLOCK, DA_HEADS, 2, DA_HEAD_DIM), 1, 0)
    kpos = jnp.arange(S)
    vf = v.astype(jnp.float32)
    scale = DA_HEAD_DIM ** -0.5

    def block(args):
        qi, bi = args
        s = jnp.einsum('bqhcd,bkhcd->bhcqk', qi, k).astype(jnp.float32) * scale
        qpos = bi * Q_BLOCK + jnp.arange(Q_BLOCK)
        limit = (qpos // CHUNK + 1) * CHUNK
        mask = kpos[None, :] < limit[:, None]
        s = jnp.where(mask, s, -jnp.inf)
        w = _diff_weights(s, lam)
        return jnp.einsum('bhqk,bkhd->bqhd', w, vf)

    o = lax.map(block, (qb, jnp.arange(nb)))
    return jnp.moveaxis(o, 0, 1).reshape(B, S, DA_HEADS, DA_V_DIM)


def _diff_attn_sample(q, k_all, v_all, lam):
    s = jnp.einsum('bqhcd,bkhcd->bhcqk', q, k_all).astype(jnp.float32) * (DA_HEAD_DIM ** -0.5)
    w = _diff_weights(s, lam)
    return jnp.einsum('bhqk,bkhd->bqhd', w, v_all.astype(jnp.float32))


def _gla_chunk(S, inp):
    q, k, v, g = inp
    C = q.shape[2]
    b = jnp.cumsum(g, axis=2)
    o_inter = jnp.einsum('bhtk,bhkv->bhtv', q * jnp.exp(b), S)
    causal = jnp.tril(jnp.ones((C, C), dtype=bool))
    rel = b[:, :, :, None, :] - b[:, :, None, :, :]
    decay = jnp.exp(jnp.where(causal[:, :, None], rel, -jnp.inf))
    A = jnp.sum(q[:, :, :, None, :] * k[:, :, None, :, :] * decay, axis=-1)
    o = o_inter + jnp.einsum('bhts,bhsv->bhtv', A, v)
    b_last = b[:, :, -1:, :]
    S_new = (jnp.exp(b_last[:, :, 0, :])[..., None] * S
             + jnp.einsum('bhsk,bhsv->bhkv', k * jnp.exp(b_last - b), v))
    return S_new, o


def _gla(q, k, v, g, S0):
    B, H, L = q.shape[0], q.shape[1], q.shape[2]
    C = CHUNK if L % CHUNK == 0 else L
    n = L // C

    def split(t):
        return jnp.moveaxis(t.reshape(B, H, n, C, t.shape[-1]), 2, 0)

    S, o = lax.scan(_gla_chunk, S0, (split(q), split(k), split(v), split(g)))
    o = jnp.moveaxis(o, 0, 2).reshape(B, H, L, GLA_DV)
    return o, S


def _layer(x, pos, kv_past, S0, conv_past, li, w_in, w_gk2, b_gk2, lq1, lk1, lq2, lk2,
           da_norm_w, gla_norm_w, w_o, pre_mix_w, post_mix_w, pre_ffn_w, post_ffn_w,
           w_up, conv_w, conv_b, w_down):
    B, L = x.shape[0], x.shape[1]
    f32 = jnp.float32
    xn = _rmsnorm(x, pre_mix_w)
    proj = xn @ w_in
    sizes = (DA_QK, DA_QK, DA_V, GLA_QK, GLA_QK, GLA_V, GLA_V, GLA_RANK, D_MODEL)
    cuts = []
    acc = 0
    for sz in sizes:
        acc += sz
        cuts.append(acc)
    qa, ka, va, qg, kg, vg, rg, lr, ga, gb = jnp.split(proj, cuts, axis=-1)

    qa = _rope(qa.reshape(B, L, DA_HEADS, 2, DA_HEAD_DIM), pos)
    ka = _rope(ka.reshape(B, L, DA_HEADS, 2, DA_HEAD_DIM), pos)
    va = va.reshape(B, L, DA_HEADS, DA_V_DIM)
    lam_init = 0.8 - 0.6 * math.exp(-0.3 * li)
    lam = (jnp.exp(jnp.sum(lq1.astype(f32) * lk1.astype(f32)))
           - jnp.exp(jnp.sum(lq2.astype(f32) * lk2.astype(f32))) + lam_init)
    if kv_past is None:
        oa = _diff_attn_prompt(qa, ka, va, lam)
    else:
        k_past, v_past = kv_past
        P = k_past.shape[1]
        k_all = jnp.concatenate(
            [k_past.reshape(B, P, DA_HEADS, 2, DA_HEAD_DIM).astype(ka.dtype), ka], axis=1)
        v_all = jnp.concatenate([v_past.astype(va.dtype), va], axis=1)
        oa = _diff_attn_sample(qa, k_all, v_all, lam)
    oa = (_rmsnorm(oa, da_norm_w) * (1.0 - lam_init)).reshape(B, L, D_MODEL).astype(x.dtype)
    new_k = ka.reshape(B, L, DA_HEADS, DA_K_ROW)
    new_v = va

    def heads(t, d):
        return t.reshape(B, L, GLA_HEADS, d).transpose(0, 2, 1, 3).astype(f32)

    gk = jax.nn.log_sigmoid((lr @ w_gk2 + b_gk2).astype(f32)) / GLA_GATE_NORM
    og, S_new = _gla(heads(qg, GLA_DK) * (GLA_DK ** -0.5), heads(kg, GLA_DK),
                     heads(vg, GLA_DV), heads(gk, GLA_DK), S0.astype(f32))
    og = og.transpose(0, 2, 1, 3)
    og = _rmsnorm(og, gla_norm_w) * jax.nn.silu(rg.astype(f32)).reshape(B, L, GLA_HEADS, GLA_DV)
    ob = og.reshape(B, L, D_MODEL).astype(x.dtype)

    merged = jax.nn.sigmoid(ga) * oa + jax.nn.sigmoid(gb) * ob
    h = x + _rmsnorm(merged @ w_o, post_mix_w)

    hn = _rmsnorm(h, pre_ffn_w)
    u, g = jnp.split(hn @ w_up, [D_FF], axis=-1)
    gpad = jnp.concatenate([conv_past.astype(g.dtype), g], axis=1)
    gc = conv_b + sum(conv_w[j] * gpad[:, j:j + L] for j in range(CONV_W))
    ffn = (jax.nn.gelu(gc, approximate=True) * u) @ w_down
    out = h + _rmsnorm(ffn, post_ffn_w)
    return out, new_k, new_v, S_new, gpad[:, -(CONV_W - 1):]


def setup_inputs(seed: int = 0) -> dict:
    key = jax.random.key(seed)
    ks = jax.random.split(key, 32)
    nrm = jax.random.normal
    f32 = jnp.float32
    return {
        "x_prompt": nrm(ks[0], (BATCH, SEQ, D_MODEL), f32),
        "x_sample": nrm(ks[1], (DEC_BATCH, DEC_SEQ, D_MODEL), f32),
        "cache_k": nrm(ks[2], (DEPTH, DEC_BATCH, PAST_LEN, DA_HEADS, DA_K_ROW), f32),
        "cache_v": nrm(ks[3], (DEPTH, DEC_BATCH, PAST_LEN, DA_HEADS, DA_V_DIM), f32),
        "state_gla": nrm(ks[4], (DEPTH, DEC_BATCH, GLA_HEADS, GLA_DK, GLA_DV), f32),
        "state_conv": nrm(ks[5], (DEPTH, DEC_BATCH, CONV_W - 1, D_FF), f32),
        "w_in": nrm(ks[6], (DEPTH, D_MODEL, D_IN), f32) * D_MODEL ** -0.5,
        "w_gk2": nrm(ks[7], (DEPTH, GLA_RANK, GLA_QK), f32) * GLA_RANK ** -0.5,
        "b_gk2": nrm(ks[8], (DEPTH, GLA_QK), f32) * 0.1,
        "lambda_q1": nrm(ks[9], (DEPTH, DA_HEAD_DIM), f32) * 0.1,
        "lambda_k1": nrm(ks[10], (DEPTH, DA_HEAD_DIM), f32) * 0.1,
        "lambda_q2": nrm(ks[11], (DEPTH, DA_HEAD_DIM), f32) * 0.1,
        "lambda_k2": nrm(ks[12], (DEPTH, DA_HEAD_DIM), f32) * 0.1,
        "da_norm_w": 1.0 + 0.05 * nrm(ks[13], (DEPTH, DA_V_DIM), f32),
        "gla_norm_w": 1.0 + 0.05 * nrm(ks[14], (DEPTH, GLA_DV), f32),
        "w_o": nrm(ks[15], (DEPTH, D_MODEL, D_MODEL), f32) * D_MODEL ** -0.5,
        "pre_mix_w": 1.0 + 0.05 * nrm(ks[16], (DEPTH, D_MODEL), f32),
        "post_mix_w": 1.0 + 0.05 * nrm(ks[17], (DEPTH, D_MODEL), f32),
        "pre_ffn_w": 1.0 + 0.05 * nrm(ks[18], (DEPTH, D_MODEL), f32),
        "post_ffn_w": 1.0 + 0.05 * nrm(ks[19], (DEPTH, D_MODEL), f32),
        "w_up": nrm(ks[20], (DEPTH, D_MODEL, 2 * D_FF), f32) * D_MODEL ** -0.5,
        "conv_w": nrm(ks[21], (DEPTH, CONV_W, D_FF), f32) * CONV_W ** -0.5,
        "conv_b": nrm(ks[22], (DEPTH, D_FF), f32) * 0.02,
        "w_down": nrm(ks[23], (DEPTH, D_FF, D_MODEL), f32) * D_FF ** -0.5,
    }


def reference(x_prompt, x_sample, cache_k, cache_v, state_gla, state_conv,
              w_in, w_gk2, b_gk2, lambda_q1, lambda_k1, lambda_q2, lambda_k2,
              da_norm_w, gla_norm_w, w_o, pre_mix_w, post_mix_w, pre_ffn_w, post_ffn_w,
              w_up, conv_w, conv_b, w_down):
    B, S = x_prompt.shape[0], x_prompt.shape[1]
    L = x_sample.shape[1]
    P = cache_k.shape[2]
    pos_p = jnp.arange(S)
    pos_s = P + jnp.arange(L)
    hp, hs = x_prompt, x_sample
    kp_l, vp_l, sp_l, cp_l = [], [], [], []
    ks_l, vs_l, ss_l, cs_l = [], [], [], []
    for li in range(DEPTH):
        params = (w_in[li], w_gk2[li], b_gk2[li], lambda_q1[li], lambda_k1[li],
                  lambda_q2[li], lambda_k2[li], da_norm_w[li], gla_norm_w[li], w_o[li],
                  pre_mix_w[li], post_mix_w[li], pre_ffn_w[li], post_ffn_w[li],
                  w_up[li], conv_w[li], conv_b[li], w_down[li])
        S0_p = jnp.zeros((B, GLA_HEADS, GLA_DK, GLA_DV), jnp.float32)
        conv0_p = jnp.zeros((B, CONV_W - 1, D_FF), x_prompt.dtype)
        hp, kp, vp, sp, cp = _layer(hp, pos_p, None, S0_p, conv0_p, li, *params)
        hs, kn, vn, sn, cn = _layer(hs, pos_s, (cache_k[li], cache_v[li]),
                                    state_gla[li], state_conv[li], li, *params)
        kp_l.append(kp); vp_l.append(vp); sp_l.append(sp); cp_l.append(cp)
        ks_l.append(kn); vs_l.append(vn); ss_l.append(sn.astype(state_gla.dtype)); cs_l.append(cn)
    return (hp, hs,
            jnp.stack(kp_l), jnp.stack(vp_l), jnp.stack(sp_l), jnp.stack(cp_l),
            jnp.stack(ks_l), jnp.stack(vs_l), jnp.stack(ss_l), jnp.stack(cs_l))
```

```python
import functools
import math

import jax
import jax.numpy as jnp
from jax import lax
from jax.experimental import pallas as pl
from jax.experimental.pallas import tpu as pltpu

F32 = jnp.float32
BF16 = jnp.bfloat16

D_MODEL = 1024
CHUNK = 64
DA_HEADS = 8
DA_HEAD_DIM = 64
DA_ROW = 2 * DA_HEAD_DIM
ROPE_DIM = DA_HEAD_DIM // 4
ROPE_HALF = ROPE_DIM // 2
ROPE_THETA = 500000.0
GLA_HEADS = 4
GLA_DK = 128
GLA_DV = 256
GLA_RANK = 16
GLA_GATE_NORM = 16.0
D_FF = 2816
CONV_W = 3
EPS = 1e-6

LANES = 128
SUBLANES = 8
VMEM_LIMIT = 56 * 1024 * 1024

_C_QA, _C_KA, _C_VA = 0, 1024, 2048
_C_QG, _C_KG, _C_VG, _C_RG = 3072, 3584, 4096, 5120
_C_GA, _C_GB, _C_LR = 6144, 7168, 8192
_W_COLS = _C_LR + LANES

_NT = (((1,), (1,)), ((), ()))
_TN = (((0,), (0,)), ((), ()))


def _rms(x, w):
    return x * lax.rsqrt(jnp.mean(x * x, axis=-1, keepdims=True) + EPS) * w


def _params(n_axes):
    return pltpu.CompilerParams(
        dimension_semantics=("arbitrary",) * n_axes, vmem_limit_bytes=VMEM_LIMIT)


def _resident(shape):
    nd = len(shape)
    return pl.BlockSpec(shape, lambda *_: (0,) * nd, pipeline_mode=pl.Buffered(1))


def _inproj_kernel(x_ref, nw_ref, w_ref, cos_ref, sna_ref, snb_ref, wgk_ref, bgk_ref,
                   q_ref, k_ref, v_ref, qg_ref, kg_ref, vg_ref, rg_ref, gk_ref, ga_ref, gb_ref):
    xb = _rms(x_ref[...], nw_ref[...]).astype(BF16)

    def proj(c0, n):
        return jnp.dot(xb, w_ref[:, c0:c0 + n], preferred_element_type=F32)

    cos, sna, snb = cos_ref[...], sna_ref[...], snb_ref[...]

    def rope(y):
        return (y * cos + pltpu.roll(y, LANES - ROPE_HALF, 1) * sna
                + pltpu.roll(y, ROPE_HALF, 1) * snb)

    for c0, dst, cast in ((_C_QA, q_ref, True), (_C_KA, k_ref, False)):
        for blk in range(2):
            y = proj(c0 + blk * 512, 512)
            for h in range(4):
                r = rope(y[:, h * DA_ROW:(h + 1) * DA_ROW])
                col = blk * 512 + h * DA_ROW
                dst[:, col:col + DA_ROW] = r.astype(BF16) if cast else r
    for blk in range(2):
        v_ref[:, blk * 512:(blk + 1) * 512] = proj(_C_VA + blk * 512, 512)
    qg_ref[...] = (proj(_C_QG, 512) * (GLA_DK ** -0.5)).astype(BF16)
    kg_ref[...] = proj(_C_KG, 512).astype(BF16)
    for c0, dst in ((_C_VG, vg_ref), (_C_RG, rg_ref), (_C_GA, ga_ref), (_C_GB, gb_ref)):
        for blk in range(2):
            dst[:, blk * 512:(blk + 1) * 512] = proj(c0 + blk * 512, 512).astype(BF16)
    lr = proj(_C_LR, LANES)
    z = jnp.dot(lr.astype(BF16), wgk_ref[...], preferred_element_type=F32) + bgk_ref[...]
    gk_ref[...] = (jnp.minimum(z, 0.0) - jnp.log1p(jnp.exp(-jnp.abs(z)))) * (1.0 / GLA_GATE_NORM)


def _rope_tables(pos):
    inv = ROPE_THETA ** (-jnp.arange(ROPE_HALF, dtype=F32) * 2.0 / ROPE_DIM)
    ang = pos.astype(F32)[:, None] * inv[None, :]
    cos, sin = jnp.cos(ang), jnp.sin(ang)
    n = pos.shape[0]
    rest = DA_HEAD_DIM - ROPE_DIM
    z8 = jnp.zeros((n, ROPE_HALF), F32)
    cos64 = jnp.concatenate([cos, cos, jnp.ones((n, rest), F32)], axis=-1)
    sna64 = jnp.concatenate([-sin, z8, jnp.zeros((n, rest), F32)], axis=-1)
    snb64 = jnp.concatenate([z8, sin, jnp.zeros((n, rest), F32)], axis=-1)
    return tuple(jnp.tile(t, (1, 2)) for t in (cos64, sna64, snb64))


def _inproj(x2d, seq_len, pos0, nw, w_cat, wgk, bgk):
    T = x2d.shape[0]
    tm = min(256, T)
    period = max(seq_len, tm)
    pos = pos0 + jnp.arange(period, dtype=jnp.int32) % seq_len
    cos, sna, snb = _rope_tables(pos)
    nper = period // tm
    row = lambda w: pl.BlockSpec((tm, w), lambda i: (i, 0))
    tab = pl.BlockSpec((tm, LANES), lambda i: (i % nper, 0))
    widths = (1024, 1024, 1024, 512, 512, 1024, 1024, 512, 1024, 1024)
    dtypes = (BF16, F32, F32, BF16, BF16, BF16, BF16, F32, BF16, BF16)
    return pl.pallas_call(
        _inproj_kernel,
        grid=(T // tm,),
        in_specs=[row(D_MODEL), _resident((1, D_MODEL)), _resident((D_MODEL, _W_COLS)),
                  tab, tab, tab, _resident((LANES, 512)), _resident((1, 512))],
        out_specs=[row(w) for w in widths],
        out_shape=[jax.ShapeDtypeStruct((T, w), d) for w, d in zip(widths, dtypes)],
        compiler_params=_params(1),
        name="inproj",
    )(x2d, nw, w_cat, cos, sna, snb, wgk, bgk)


def _split_q(q):
    lane = lax.broadcasted_iota(jnp.int32, q.shape, 1)
    qs = q * jnp.asarray(DA_HEAD_DIM ** -0.5, BF16)
    zero = jnp.zeros_like(qs)
    return jnp.concatenate([jnp.where(lane < DA_HEAD_DIM, qs, zero),
                            jnp.where(lane < DA_HEAD_DIM, zero, qs)], axis=0)


def _lambda(lq1, lk1, lq2, lk2, lam_init):
    return (jnp.exp(jnp.sum(lq1[...] * lk1[...], axis=1, keepdims=True))
            - jnp.exp(jnp.sum(lq2[...] * lk2[...], axis=1, keepdims=True)) + lam_init)


def _attn_finish(o12, lam, nw, lam_init, n):
    od = o12[:n] - lam * o12[n:]
    return (_rms(od, nw) * (1.0 - lam_init)).astype(BF16)


def _attn_prompt_kernel(q_ref, k_ref, v_ref, lq1, lk1, lq2, lk2, nw_ref, o_ref,
                        kb_ref, vb_ref, m_ref, l_ref, acc_ref, *, tq, lam_init):
    i = pl.program_id(2)
    S = k_ref.shape[0]

    @pl.when(i == 0)
    def _():
        def cvt(c, carry):
            r = pl.multiple_of(c * tq, tq)
            kb_ref[pl.ds(r, tq), :] = k_ref[pl.ds(r, tq), :].astype(BF16)
            vb_ref[pl.ds(r, tq), :] = v_ref[pl.ds(r, tq), :].astype(BF16)
            return carry
        lax.fori_loop(0, S // tq, cvt, 0)

    q2 = _split_q(q_ref[...])
    m_ref[...] = jnp.full(m_ref.shape, -jnp.inf, F32)
    l_ref[...] = jnp.zeros(l_ref.shape, F32)
    acc_ref[...] = jnp.zeros(acc_ref.shape, F32)

    def step(j, masked):
        r = pl.multiple_of(j * tq, tq)
        kj = kb_ref[pl.ds(r, tq), :]
        vj = vb_ref[pl.ds(r, tq), :]
        s = lax.dot_general(q2, kj, _NT, preferred_element_type=F32)
        if masked:
            row = lax.broadcasted_iota(jnp.int32, s.shape, 0)
            col = lax.broadcasted_iota(jnp.int32, s.shape, 1)
            visible = (col // CHUNK) <= ((row % tq) // CHUNK)
            s = jnp.where(visible, s, -jnp.inf)
        m_prev = m_ref[...]
        m_new = jnp.maximum(m_prev, jnp.max(s, axis=1, keepdims=True))
        alpha = jnp.exp(m_prev - m_new)
        p = jnp.exp(s - pltpu.repeat(m_new, tq // LANES, axis=1))
        l_ref[...] = alpha * l_ref[...] + jnp.sum(p, axis=1, keepdims=True)
        acc_ref[...] = alpha * acc_ref[...] + jnp.dot(p.astype(BF16), vj, preferred_element_type=F32)
        m_ref[...] = m_new

    def body(j, carry):
        step(j, False)
        return carry
    lax.fori_loop(0, i, body, 0)
    step(i, True)

    lam = _lambda(lq1, lk1, lq2, lk2, lam_init)
    o_ref[...] = _attn_finish(acc_ref[...] / l_ref[...], lam, nw_ref[...], lam_init, tq)


def _attn_prompt(q, k, v, lam_p, nw, lam_init):
    B, S, _ = q.shape
    tq = min(256, S)
    lam_spec = _resident((1, DA_HEAD_DIM))
    return pl.pallas_call(
        functools.partial(_attn_prompt_kernel, tq=tq, lam_init=lam_init),
        grid=(B, DA_HEADS, S // tq),
        in_specs=[pl.BlockSpec((None, tq, DA_ROW), lambda b, h, i: (b, i, h)),
                  pl.BlockSpec((None, S, DA_ROW), lambda b, h, i: (b, 0, h)),
                  pl.BlockSpec((None, S, DA_ROW), lambda b, h, i: (b, 0, h)),
                  lam_spec, lam_spec, lam_spec, lam_spec, _resident((1, DA_ROW))],
        out_specs=pl.BlockSpec((None, tq, DA_ROW), lambda b, h, i: (b, i, h)),
        out_shape=jax.ShapeDtypeStruct((B, S, D_MODEL), BF16),
        scratch_shapes=[pltpu.VMEM((S, DA_ROW), BF16), pltpu.VMEM((S, DA_ROW), BF16),
                        pltpu.VMEM((2 * tq, LANES), F32), pltpu.VMEM((2 * tq, LANES), F32),
                        pltpu.VMEM((2 * tq, DA_ROW), F32)],
        compiler_params=_params(3),
        name="attn_prompt",
    )(q, k, v, *lam_p, nw)


def _attn_sample_kernel(q_ref, kn_ref, vn_ref, ck_ref, cv_ref, lq1, lk1, lq2, lk2, nw_ref, o_ref,
                        *, lam_init):
    L = q_ref.shape[0]
    q2 = _split_q(q_ref[...])
    s_p = lax.dot_general(q2, ck_ref[...].astype(BF16), _NT, preferred_element_type=F32)
    s_n = lax.dot_general(q2, kn_ref[...].astype(BF16), _NT, preferred_element_type=F32)
    m = jnp.maximum(jnp.max(s_p, axis=1, keepdims=True), jnp.max(s_n, axis=1, keepdims=True))
    p_p = jnp.exp(s_p - m)
    p_n = jnp.exp(s_n - m)
    l = jnp.sum(p_p, axis=1, keepdims=True) + jnp.sum(p_n, axis=1, keepdims=True)
    acc = (jnp.dot(p_p.astype(BF16), cv_ref[...].astype(BF16), preferred_element_type=F32)
           + jnp.dot(p_n.astype(BF16), vn_ref[...].astype(BF16), preferred_element_type=F32))
    lam = _lambda(lq1, lk1, lq2, lk2, lam_init)
    o_ref[...] = _attn_finish(acc / l, lam, nw_ref[...], lam_init, L)


def _attn_sample(q, kn, vn, cache_k, cache_v, li, lam_p, nw, lam_init):
    B, L, _ = q.shape
    P = cache_k.shape[2]
    new = pl.BlockSpec((None, L, DA_ROW), lambda b, h: (b, 0, h))
    past = pl.BlockSpec((None, None, P, DA_ROW), lambda b, h: (li, b, 0, h))
    lam_spec = _resident((1, DA_HEAD_DIM))
    return pl.pallas_call(
        functools.partial(_attn_sample_kernel, lam_init=lam_init),
        grid=(B, DA_HEADS),
        in_specs=[new, new, new, past, past, lam_spec, lam_spec, lam_spec, lam_spec,
                  _resident((1, DA_ROW))],
        out_specs=new,
        out_shape=jax.ShapeDtypeStruct((B, L, D_MODEL), BF16),
        compiler_params=_params(2),
        name="attn_sample",
    )(q, kn, vn, cache_k, cache_v, *lam_p, nw)


def _gla_kernel(q_ref, k_ref, v_ref, g_ref, r_ref, s0_ref, nw_ref, o_ref, sout_ref,
                qs_ref, ks_ref, vs_ref, os_ref, S_ref, *, C, nchunk):
    A = C // SUBLANES
    i = pl.program_id(2)

    @pl.when(i == 0)
    def _():
        S_ref[...] = s0_ref[...]

    qs_ref[...] = q_ref[...].astype(F32)
    ks_ref[...] = k_ref[...].astype(F32)
    for half in range(GLA_DV // LANES):
        vs_ref[half] = v_ref[:, half * LANES:(half + 1) * LANES].astype(F32)

    def time_of(r):
        return (r % SUBLANES) * A + r // SUBLANES

    t_r = time_of(lax.broadcasted_iota(jnp.int32, (C, C), 0))
    t_c = time_of(lax.broadcasted_iota(jnp.int32, (C, C), 1))
    t_k = time_of(lax.broadcasted_iota(jnp.int32, (C, GLA_DK), 0))
    halves = []
    hsz = C // 2
    while hsz >= A:
        halves.append(hsz)
        hsz //= 2
    sel = [t_c <= t_r] + [t_c <= (t_r // (2 * h)) * (2 * h) + h - 1 for h in halves]
    sel = jnp.concatenate([jnp.where(m, 1.0, 0.0) for m in sel], axis=0).astype(BF16)
    same_parent = [(t_r // (2 * h)) == (t_c // (2 * h)) for h in halves]
    second_half = [(t_k // h) % 2 == 1 for h in halves]
    r8 = lax.broadcasted_iota(jnp.int32, (SUBLANES, C), 0)
    c8 = lax.broadcasted_iota(jnp.int32, (SUBLANES, C), 1)
    ones_cv = jnp.ones((C, GLA_DV), BF16)

    def load(ref, base):
        return jnp.concatenate(
            [ref[pl.ds(base + a, SUBLANES, stride=A), :] for a in range(A)], axis=0)

    for c in range(nchunk):
        base = c * C
        q, k, g = load(qs_ref, base), load(ks_ref, base), load(g_ref, base)
        v = jnp.concatenate([load(vs_ref.at[half], base) for half in range(GLA_DV // LANES)],
                            axis=1).astype(BF16)
        g_hi = g.astype(BF16)
        g_lo = (g - g_hi.astype(F32)).astype(BF16)
        sums = jnp.dot(sel, jnp.concatenate([g_hi, g_lo], axis=1), preferred_element_type=F32)
        sums = sums[:, :GLA_DK] + sums[:, GLA_DK:]
        b = sums[:C]
        S = S_ref[...]
        o = jnp.dot((q * jnp.exp(b)).astype(BF16), S.astype(BF16), preferred_element_type=F32)

        amat = jnp.zeros((C, C), F32)
        for lvl, h in enumerate(halves):
            x = b - sums[(lvl + 1) * C:(lvl + 2) * C]
            sec = second_half[lvl]
            qh = jnp.where(sec, q * jnp.exp(jnp.where(sec, x, 0.0)), 0.0).astype(BF16)
            kh = jnp.where(sec, 0.0, k * jnp.exp(jnp.where(sec, 0.0, -x))).astype(BF16)
            al = lax.dot_general(qh, kh, _NT, preferred_element_type=F32)
            amat = amat + jnp.where(same_parent[lvl], al, 0.0)
        diag = []
        for a in range(A):
            ra = slice(a * SUBLANES, (a + 1) * SUBLANES)
            acc = jnp.zeros((SUBLANES, C), F32)
            for a2 in range(a + 1):
                rb = slice(a2 * SUBLANES, (a2 + 1) * SUBLANES)
                e = q[ra] * k[rb] * jnp.exp(b[ra] - b[rb])
                acc = acc + jnp.where(c8 == a2 * SUBLANES + r8,
                                      jnp.sum(e, axis=1, keepdims=True), 0.0)
            diag.append(acc)
        amat = amat + jnp.concatenate(diag, axis=0)
        o = o + jnp.dot(amat.astype(BF16), v, preferred_element_type=F32)

        kdec = (k * jnp.exp(b[C - 1:C] - b)).astype(BF16)
        upd = lax.dot_general(kdec, v, _TN, preferred_element_type=F32)
        decay = (lax.dot_general(g_hi, ones_cv, _TN, preferred_element_type=F32)
                 + lax.dot_general(g_lo, ones_cv, _TN, preferred_element_type=F32))
        S_ref[...] = jnp.exp(decay) * S + upd
        for a in range(A):
            for half in range(GLA_DV // LANES):
                os_ref[half, pl.ds(base + a, SUBLANES, stride=A), :] = (
                    o[a * SUBLANES:(a + 1) * SUBLANES, half * LANES:(half + 1) * LANES])

    og = jnp.concatenate([os_ref[half] for half in range(GLA_DV // LANES)], axis=1)
    y = _rms(og, nw_ref[...])
    r = r_ref[...].astype(F32)
    o_ref[...] = (y * (r * jax.nn.sigmoid(r))).astype(BF16)

    @pl.when(i == pl.num_programs(2) - 1)
    def _():
        sout_ref[...] = S_ref[...]


def _gla(qg, kg, vg, gk, rg, s0, nw):
    B, L, _ = qg.shape
    C = CHUNK if L % CHUNK == 0 else L
    tc = min(256, L)
    dk = lambda b, h, i: (b, i, h)
    st = pl.BlockSpec((None, None, GLA_DK, GLA_DV), lambda b, h, i: (b, h, 0, 0))
    return pl.pallas_call(
        functools.partial(_gla_kernel, C=C, nchunk=tc // C),
        grid=(B, GLA_HEADS, L // tc),
        in_specs=[pl.BlockSpec((None, tc, GLA_DK), dk), pl.BlockSpec((None, tc, GLA_DK), dk),
                  pl.BlockSpec((None, tc, GLA_DV), dk), pl.BlockSpec((None, tc, GLA_DK), dk),
                  pl.BlockSpec((None, tc, GLA_DV), dk), st, _resident((1, GLA_DV))],
        out_specs=[pl.BlockSpec((None, tc, GLA_DV), dk), st],
        out_shape=[jax.ShapeDtypeStruct((B, L, D_MODEL), BF16),
                   jax.ShapeDtypeStruct((B, GLA_HEADS, GLA_DK, GLA_DV), F32)],
        scratch_shapes=[pltpu.VMEM((tc, GLA_DK), F32), pltpu.VMEM((tc, GLA_DK), F32),
                        pltpu.VMEM((GLA_DV // LANES, tc, LANES), F32),
                        pltpu.VMEM((GLA_DV // LANES, tc, LANES), F32),
                        pltpu.VMEM((GLA_DK, GLA_DV), F32)],
        compiler_params=_params(3),
        name="gla",
    )(qg, kg, vg, gk, rg, s0, nw)


def _merge_kernel(oa_ref, ob_ref, ga_ref, gb_ref, x_ref, wo_ref, nw_ref, h_ref):
    f = lambda r: r[...].astype(F32)
    merged = jax.nn.sigmoid(f(ga_ref)) * f(oa_ref) + jax.nn.sigmoid(f(gb_ref)) * f(ob_ref)
    y = jnp.dot(merged.astype(BF16), wo_ref[...], preferred_element_type=F32)
    h_ref[...] = x_ref[...] + _rms(y, nw_ref[...])


def _merge(oa, ob, ga, gb, x2d, wo, nw):
    T = x2d.shape[0]
    tm = min(512, T)
    row = pl.BlockSpec((tm, D_MODEL), lambda i: (i, 0))
    return pl.pallas_call(
        _merge_kernel,
        grid=(T // tm,),
        in_specs=[row, row, row, row, row, _resident((D_MODEL, D_MODEL)), _resident((1, D_MODEL))],
        out_specs=row,
        out_shape=jax.ShapeDtypeStruct((T, D_MODEL), F32),
        compiler_params=_params(1),
        name="merge",
    )(oa, ob, ga, gb, x2d, wo, nw)


_FF_BLK = 256


def _ffn_kernel(h_ref, nw1_ref, wup_ref, cw_ref, cb_ref, wdn_ref, past_ref, nw2_ref,
                out_ref, cs_ref, act_ref, carry_ref, *, tiles_per_seq):
    i = pl.program_id(0)
    tm = h_ref.shape[0]

    @pl.when(i % tiles_per_seq == 0)
    def _():
        carry_ref[SUBLANES - 2:SUBLANES, :] = past_ref[...]

    h = h_ref[...]
    hb = _rms(h, nw1_ref[...]).astype(BF16)
    rowid = lax.broadcasted_iota(jnp.int32, (tm, _FF_BLK), 0)
    for blk in range(D_FF // _FF_BLK):
        cols = slice(blk * _FF_BLK, (blk + 1) * _FF_BLK)
        u = jnp.dot(hb, wup_ref[:, cols], preferred_element_type=F32)
        g = jnp.dot(hb, wup_ref[:, D_FF + blk * _FF_BLK:D_FF + (blk + 1) * _FF_BLK],
                    preferred_element_type=F32)
        prev2 = carry_ref[SUBLANES - 2:SUBLANES - 1, cols]
        prev1 = carry_ref[SUBLANES - 1:SUBLANES, cols]
        g1 = jnp.where(rowid == 0, prev1, pltpu.roll(g, 1, 0))
        g2 = jnp.where(rowid == 0, prev2, jnp.where(rowid == 1, prev1, pltpu.roll(g, 2, 0)))
        gc = cb_ref[:, cols] + cw_ref[0:1, cols] * g2 + cw_ref[1:2, cols] * g1 + cw_ref[2:3, cols] * g
        act_ref[:, cols] = (jax.nn.gelu(gc, approximate=True) * u).astype(BF16)
        carry_ref[:, cols] = g[tm - SUBLANES:tm, :]
        cs_ref[:, cols] = g[tm - (CONV_W - 1):tm, :]
    ffn = jnp.dot(act_ref[...], wdn_ref[...], preferred_element_type=F32)
    out_ref[...] = h + _rms(ffn, nw2_ref[...])


def _ffn(h2d, seq_len, nw1, wup, cw, cb, wdn, conv_past, nw2):
    T = h2d.shape[0]
    nseq = T // seq_len
    tm = min(512, seq_len)
    tps = seq_len // tm
    row = pl.BlockSpec((tm, D_MODEL), lambda i: (i, 0))
    st = pl.BlockSpec((None, CONV_W - 1, D_FF), lambda i: (i // tps, 0, 0))
    return pl.pallas_call(
        functools.partial(_ffn_kernel, tiles_per_seq=tps),
        grid=(T // tm,),
        in_specs=[row, _resident((1, D_MODEL)), _resident((D_MODEL, 2 * D_FF)),
                  _resident((CONV_W, D_FF)), _resident((1, D_FF)), _resident((D_FF, D_MODEL)),
                  st, _resident((1, D_MODEL))],
        out_specs=[row, st],
        out_shape=[jax.ShapeDtypeStruct((T, D_MODEL), F32),
                   jax.ShapeDtypeStruct((nseq, CONV_W - 1, D_FF), F32)],
        scratch_shapes=[pltpu.VMEM((tm, D_FF), BF16), pltpu.VMEM((SUBLANES, D_FF), F32)],
        compiler_params=_params(1),
        name="ffn",
    )(h2d, nw1, wup, cw, cb, wdn, conv_past, nw2)


def _layer(x2d, nseq, seq_len, pos0, caches, li, s0, conv_past, p):
    (q, k, v, qg, kg, vg, rg, gk, ga, gb) = _inproj(
        x2d, seq_len, pos0, p["pre_mix_w"], p["w_cat"], p["w_gk2"], p["b_gk2"])
    lam_init = 0.8 - 0.6 * math.exp(-0.3 * li)
    lam_p = (p["lq1"], p["lk1"], p["lq2"], p["lk2"])
    sq = lambda t: t.reshape(nseq, seq_len, t.shape[-1])
    if caches is None:
        oa = _attn_prompt(sq(q), sq(k), sq(v), lam_p, p["da_norm_w"], lam_init)
    else:
        oa = _attn_sample(sq(q), sq(k), sq(v), caches[0], caches[1], li, lam_p,
                          p["da_norm_w"], lam_init)
    ob, s_new = _gla(sq(qg), sq(kg), sq(vg), sq(gk), sq(rg), s0, p["gla_norm_w"])
    T = x2d.shape[0]
    h = _merge(oa.reshape(T, D_MODEL), ob.reshape(T, D_MODEL), ga, gb, x2d, p["w_o"],
               p["post_mix_w"])
    out, conv_new = _ffn(h, seq_len, p["pre_ffn_w"], p["w_up"], p["conv_w"], p["conv_b"],
                         p["w_down"], conv_past, p["post_ffn_w"])
    new_k = k.reshape(nseq, seq_len, DA_HEADS, DA_ROW)
    new_v = v.reshape(nseq, seq_len, DA_HEADS, DA_ROW)
    return out, new_k, new_v, s_new, conv_new


def kernel(x_prompt, x_sample, cache_k, cache_v, state_gla, state_conv, w_in, w_gk2, b_gk2, lambda_q1, lambda_k1, lambda_q2, lambda_k2, da_norm_w, gla_norm_w, w_o, pre_mix_w, post_mix_w, pre_ffn_w, post_ffn_w, w_up, conv_w, conv_b, w_down):
    B, S, _ = x_prompt.shape
    Bs, L, _ = x_sample.shape
    depth, _, P = cache_k.shape[:3]
    ck = cache_k.reshape(depth, Bs, P, D_MODEL)
    cv = cache_v.reshape(depth, Bs, P, D_MODEL)
    hp = x_prompt.reshape(B * S, D_MODEL)
    hs = x_sample.reshape(Bs * L, D_MODEL)
    s0_p = jnp.zeros((B, GLA_HEADS, GLA_DK, GLA_DV), F32)
    conv0_p = jnp.zeros((B, CONV_W - 1, D_FF), F32)
    c_lr = _C_LR - 2 * D_MODEL + GLA_RANK
    outs_p, outs_s = [], []
    for li in range(depth):
        w = w_in[li]
        w_cat = jnp.concatenate(
            [w[:, :c_lr - GLA_RANK], w[:, c_lr:], w[:, c_lr - GLA_RANK:c_lr],
             jnp.zeros((D_MODEL, LANES - GLA_RANK), w.dtype)], axis=1).astype(BF16)
        p = dict(
            w_cat=w_cat,
            w_gk2=jnp.concatenate([w_gk2[li], jnp.zeros((LANES - GLA_RANK, w_gk2.shape[-1]),
                                                        w_gk2.dtype)], axis=0).astype(BF16),
            b_gk2=b_gk2[li][None, :],
            lq1=lambda_q1[li][None, :], lk1=lambda_k1[li][None, :],
            lq2=lambda_q2[li][None, :], lk2=lambda_k2[li][None, :],
            da_norm_w=da_norm_w[li][None, :], gla_norm_w=gla_norm_w[li][None, :],
            w_o=w_o[li].astype(BF16),
            pre_mix_w=pre_mix_w[li][None, :], post_mix_w=post_mix_w[li][None, :],
            pre_ffn_w=pre_ffn_w[li][None, :], post_ffn_w=post_ffn_w[li][None, :],
            w_up=w_up[li].astype(BF16), conv_w=conv_w[li], conv_b=conv_b[li][None, :],
            w_down=w_down[li].astype(BF16))
        hp, *rest_p = _layer(hp, B, S, 0, None, li, s0_p, conv0_p, p)
        hs, *rest_s = _layer(hs, Bs, L, P, (ck, cv), li, state_gla[li], state_conv[li], p)
        outs_p.append(rest_p)
        outs_s.append(rest_s)
    stack = lambda outs, j: jnp.stack([o[j] for o in outs])
    return (hp.reshape(B, S, D_MODEL), hs.reshape(Bs, L, D_MODEL),
            stack(outs_p, 0), stack(outs_p, 1), stack(outs_p, 2), stack(outs_p, 3),
            stack(outs_s, 0), stack(outs_s, 1), stack(outs_s, 2), stack(outs_s, 3))
```

```python
import functools
import math

import jax
import jax.numpy as jnp
from jax import lax
from jax.experimental import pallas as pl
from jax.experimental.pallas import tpu as pltpu

F32 = jnp.float32
BF16 = jnp.bfloat16

D_MODEL = 1024
CHUNK = 64
DA_HEADS = 8
DA_HEAD_DIM = 64
DA_ROW = 2 * DA_HEAD_DIM
ROPE_DIM = DA_HEAD_DIM // 4
ROPE_HALF = ROPE_DIM // 2
ROPE_THETA = 500000.0
GLA_HEADS = 4
GLA_DK = 128
GLA_DV = 256
GLA_RANK = 16
GLA_GATE_NORM = 16.0
D_FF = 2816
CONV_W = 3
EPS = 1e-6

LANES = 128
SUBLANES = 8
VMEM_LIMIT = 56 * 1024 * 1024

_C_QA, _C_KA, _C_VA = 0, 1024, 2048
_C_QG, _C_KG, _C_VG, _C_RG = 3072, 3584, 4096, 5120
_C_GA, _C_GB, _C_LR = 6144, 7168, 8192
_W_COLS = _C_LR + LANES

_Q_SCALE = DA_HEAD_DIM ** -0.5 * math.log2(math.e)

_NT = (((1,), (1,)), ((), ()))
_TN = (((0,), (0,)), ((), ()))


def _rms(x, w):
    return x * lax.rsqrt(jnp.mean(x * x, axis=-1, keepdims=True) + EPS) * w


def _params(n_axes):
    return pltpu.CompilerParams(
        dimension_semantics=("arbitrary",) * n_axes, vmem_limit_bytes=VMEM_LIMIT)


def _resident(shape):
    nd = len(shape)
    return pl.BlockSpec(shape, lambda *_: (0,) * nd, pipeline_mode=pl.Buffered(1))


def _inproj_kernel(x_ref, nw_ref, w_ref, cos_ref, sna_ref, snb_ref, wgk_ref, bgk_ref, *refs):
    (q_ref, k_ref, v_ref, kb_ref, vb_ref, qg_ref, kg_ref, vg_ref, rg_ref, gk_ref,
     ga_ref, gb_ref) = refs[-12:]
    xb = _rms(x_ref[...], nw_ref[...]).astype(BF16)

    def proj(c0, n):
        return jnp.dot(xb, w_ref[:, c0:c0 + n], preferred_element_type=F32)

    cos, sna, snb = cos_ref[...], sna_ref[...], snb_ref[...]

    def rope(y):
        return (y * cos + pltpu.roll(y, LANES - ROPE_HALF, 1) * sna
                + pltpu.roll(y, ROPE_HALF, 1) * snb)

    for blk in range(2):
        yq = proj(_C_QA + blk * 512, 512)
        yk = proj(_C_KA + blk * 512, 512)
        for h in range(4):
            cols = slice(blk * 512 + h * DA_ROW, blk * 512 + (h + 1) * DA_ROW)
            q_ref[:, cols] = (rope(yq[:, h * DA_ROW:(h + 1) * DA_ROW]) * _Q_SCALE).astype(BF16)
            kr = rope(yk[:, h * DA_ROW:(h + 1) * DA_ROW])
            k_ref[:, blk * 4 + h, :] = kr
            kb_ref[:, cols] = kr.astype(BF16)
        yv = proj(_C_VA + blk * 512, 512)
        for h in range(4):
            v_ref[:, blk * 4 + h, :] = yv[:, h * DA_ROW:(h + 1) * DA_ROW]
        vb_ref[:, blk * 512:(blk + 1) * 512] = yv.astype(BF16)
    qg_ref[...] = (proj(_C_QG, 512) * (GLA_DK ** -0.5)).astype(BF16)
    kg_ref[...] = proj(_C_KG, 512).astype(BF16)
    for c0, dst in ((_C_VG, vg_ref), (_C_RG, rg_ref), (_C_GA, ga_ref), (_C_GB, gb_ref)):
        for blk in range(2):
            dst[:, blk * 512:(blk + 1) * 512] = proj(c0 + blk * 512, 512).astype(BF16)
    lr = proj(_C_LR, LANES)
    z = jnp.dot(lr.astype(BF16), wgk_ref[...], preferred_element_type=F32) + bgk_ref[...]
    gk_ref[...] = (jnp.minimum(z, 0.0) - jnp.log1p(jnp.exp(-jnp.abs(z)))) * (1.0 / GLA_GATE_NORM)


def _rope_tables(pos):
    inv = ROPE_THETA ** (-jnp.arange(ROPE_HALF, dtype=F32) * 2.0 / ROPE_DIM)
    ang = pos.astype(F32)[:, None] * inv[None, :]
    cos, sin = jnp.cos(ang), jnp.sin(ang)
    n = pos.shape[0]
    rest = DA_HEAD_DIM - ROPE_DIM
    z8 = jnp.zeros((n, ROPE_HALF), F32)
    cos64 = jnp.concatenate([cos, cos, jnp.ones((n, rest), F32)], axis=-1)
    sna64 = jnp.concatenate([-sin, z8, jnp.zeros((n, rest), F32)], axis=-1)
    snb64 = jnp.concatenate([z8, sin, jnp.zeros((n, rest), F32)], axis=-1)
    return tuple(jnp.tile(t, (1, 2)) for t in (cos64, sna64, snb64))


def _inproj(x2d, seq_len, pos0, nw, w_cat, wgk, bgk, li, depth, kv_stack):
    T = x2d.shape[0]
    tm = min(256, T)
    period = max(seq_len, tm)
    pos = pos0 + jnp.arange(period, dtype=jnp.int32) % seq_len
    cos, sna, snb = _rope_tables(pos)
    nper = period // tm
    row = lambda w: pl.BlockSpec((tm, w), lambda i: (i, 0))
    tab = pl.BlockSpec((tm, LANES), lambda i: (i % nper, 0))
    stack = pl.BlockSpec((None, tm, DA_HEADS, DA_ROW), lambda i: (li, i, 0, 0))
    stack_shape = jax.ShapeDtypeStruct((depth, T, DA_HEADS, DA_ROW), F32)
    widths = (1024, 1024, 1024, 512, 512, 1024, 1024, 512, 1024, 1024)
    dtypes = (BF16, BF16, BF16, BF16, BF16, BF16, BF16, F32, BF16, BF16)
    rows = [row(w) for w in widths]
    shapes = [jax.ShapeDtypeStruct((T, w), d) for w, d in zip(widths, dtypes)]
    prev = () if kv_stack is None else tuple(kv_stack)
    n_in = 8
    return pl.pallas_call(
        _inproj_kernel,
        grid=(T // tm,),
        in_specs=[row(D_MODEL), _resident((1, D_MODEL)), _resident((D_MODEL, _W_COLS)),
                  tab, tab, tab, _resident((LANES, 512)), _resident((1, 512))]
                 + [pl.BlockSpec(memory_space=pl.ANY)] * len(prev),
        out_specs=rows[:1] + [stack, stack] + rows[1:],
        out_shape=shapes[:1] + [stack_shape, stack_shape] + shapes[1:],
        input_output_aliases={n_in + j: 1 + j for j in range(len(prev))},
        compiler_params=_params(1),
        name="inproj",
    )(x2d, nw, w_cat, cos, sna, snb, wgk, bgk, *prev)


def _split_q(q):
    lane = lax.broadcasted_iota(jnp.int32, q.shape, 1)
    zero = jnp.zeros_like(q)
    return jnp.concatenate([jnp.where(lane < DA_HEAD_DIM, q, zero),
                            jnp.where(lane < DA_HEAD_DIM, zero, q)], axis=0)


def _lambda(lq1, lk1, lq2, lk2, lam_init):
    return (jnp.exp(jnp.sum(lq1[...] * lk1[...], axis=1, keepdims=True))
            - jnp.exp(jnp.sum(lq2[...] * lk2[...], axis=1, keepdims=True)) + lam_init)


def _attn_finish(o12, lam, nw, lam_init, n):
    od = o12[:n] - lam * o12[n:]
    return (_rms(od, nw) * (1.0 - lam_init)).astype(BF16)


_ATTN_HEADS_PER_STEP = 4


def _attn_prompt_kernel(q_ref, k_ref, v_ref, lq1, lk1, lq2, lk2, nw_ref, o_ref,
                        q2_ref, m_ref, l_ref, acc_ref, *, tq, nh, lam_init):
    i = pl.program_id(2)
    for h in range(nh):
        q2_ref[h] = _split_q(q_ref[:, h * DA_ROW:(h + 1) * DA_ROW])
    m_ref[...] = jnp.full(m_ref.shape, -jnp.inf, F32)
    l_ref[...] = jnp.zeros(l_ref.shape, F32)
    acc_ref[...] = jnp.zeros(acc_ref.shape, F32)
    ones = jnp.ones((tq, LANES), BF16)

    def step(j, masked):
        r = pl.multiple_of(j * tq, tq)
        for h in range(nh):
            cols = slice(h * DA_ROW, (h + 1) * DA_ROW)
            kj = k_ref[pl.ds(r, tq), cols]
            vj = jnp.concatenate([v_ref[pl.ds(r, tq), cols], ones], axis=1)
            s = lax.dot_general(q2_ref[h], kj, _NT, preferred_element_type=F32)
            if masked:
                row = lax.broadcasted_iota(jnp.int32, s.shape, 0)
                col = lax.broadcasted_iota(jnp.int32, s.shape, 1)
                s = jnp.where((col // CHUNK) <= ((row % tq) // CHUNK), s, -jnp.inf)
            m_prev = m_ref[h]
            m_new = jnp.maximum(m_prev, jnp.max(s, axis=1, keepdims=True))
            alpha = jnp.exp2(m_prev - m_new)
            p = jnp.exp2(s - pltpu.repeat(m_new, tq // LANES, axis=1))
            pv = jnp.dot(p.astype(BF16), vj, preferred_element_type=F32)
            acc_ref[h] = alpha * acc_ref[h] + pv[:, :DA_ROW]
            l_ref[h] = alpha * l_ref[h] + pv[:, DA_ROW:]
            m_ref[h] = m_new

    def body(j, carry):
        step(j, False)
        return carry
    lax.fori_loop(0, i, body, 0)
    step(i, True)

    lam = _lambda(lq1, lk1, lq2, lk2, lam_init)
    for h in range(nh):
        o_ref[:, h * DA_ROW:(h + 1) * DA_ROW] = _attn_finish(
            acc_ref[h] / l_ref[h], lam, nw_ref[...], lam_init, tq)


def _attn_prompt(q, k, v, lam_p, nw, lam_init):
    B, S, _ = q.shape
    tq = min(256, S)
    nh = _ATTN_HEADS_PER_STEP
    w = nh * DA_ROW
    lam_spec = _resident((1, DA_HEAD_DIM))
    stat = pltpu.VMEM((nh, 2 * tq, LANES), F32)
    return pl.pallas_call(
        functools.partial(_attn_prompt_kernel, tq=tq, nh=nh, lam_init=lam_init),
        grid=(B, DA_HEADS // nh, S // tq),
        in_specs=[pl.BlockSpec((None, tq, w), lambda b, h, i: (b, i, h)),
                  pl.BlockSpec((None, S, w), lambda b, h, i: (b, 0, h)),
                  pl.BlockSpec((None, S, w), lambda b, h, i: (b, 0, h)),
                  lam_spec, lam_spec, lam_spec, lam_spec, _resident((1, DA_ROW))],
        out_specs=pl.BlockSpec((None, tq, w), lambda b, h, i: (b, i, h)),
        out_shape=jax.ShapeDtypeStruct((B, S, D_MODEL), BF16),
        scratch_shapes=[pltpu.VMEM((nh, 2 * tq, DA_ROW), BF16), stat, stat, stat],
        compiler_params=_params(3),
        name="attn_prompt",
    )(q, k, v, *lam_p, nw)


def _attn_sample_kernel(q_ref, kn_ref, vn_ref, ck_ref, cv_ref, lq1, lk1, lq2, lk2, nw_ref, o_ref,
                        ckb_ref, cvb_ref, *, lam_init):
    L, P = q_ref.shape[0], ck_ref.shape[0]
    flat = lambda r: r[...].reshape(r.shape[0] * DA_HEADS, DA_ROW).astype(BF16)
    ckb_ref[...] = flat(ck_ref)
    cvb_ref[...] = flat(cv_ref)
    kn, vn = flat(kn_ref), flat(vn_ref)
    head_p = lax.broadcasted_iota(jnp.int32, (2 * L, P * DA_HEADS), 1) % DA_HEADS
    head_n = lax.broadcasted_iota(jnp.int32, (2 * L, L * DA_HEADS), 1) % DA_HEADS
    lam = _lambda(lq1, lk1, lq2, lk2, lam_init)
    for h in range(DA_HEADS):
        cols = slice(h * DA_ROW, (h + 1) * DA_ROW)
        q2 = _split_q(q_ref[:, cols])
        s_p = lax.dot_general(q2, ckb_ref[...], _NT, preferred_element_type=F32)
        s_n = lax.dot_general(q2, kn, _NT, preferred_element_type=F32)
        s_p = jnp.where(head_p == h, s_p, -jnp.inf)
        s_n = jnp.where(head_n == h, s_n, -jnp.inf)
        m = jnp.maximum(jnp.max(s_p, axis=1, keepdims=True), jnp.max(s_n, axis=1, keepdims=True))
        p_p = jnp.exp2(s_p - m)
        p_n = jnp.exp2(s_n - m)
        l = jnp.sum(p_p, axis=1, keepdims=True) + jnp.sum(p_n, axis=1, keepdims=True)
        acc = (jnp.dot(p_p.astype(BF16), cvb_ref[...], preferred_element_type=F32)
               + jnp.dot(p_n.astype(BF16), vn, preferred_element_type=F32))
        o_ref[:, cols] = _attn_finish(acc / l, lam, nw_ref[...], lam_init, L)


def _attn_sample(q, k_stack, v_stack, cache_k, cache_v, li, lam_p, nw, lam_init):
    B, L, _ = q.shape
    P = cache_k.shape[2]
    row = pl.BlockSpec((None, L, D_MODEL), lambda b: (b, 0, 0))
    new = pl.BlockSpec((None, L, DA_HEADS, DA_ROW), lambda b: (li, b, 0, 0))
    past = pl.BlockSpec((None, None, P, DA_HEADS, DA_ROW), lambda b: (li, b, 0, 0, 0))
    lam_spec = _resident((1, DA_HEAD_DIM))
    return pl.pallas_call(
        functools.partial(_attn_sample_kernel, lam_init=lam_init),
        grid=(B,),
        in_specs=[row, new, new, past, past, lam_spec, lam_spec, lam_spec, lam_spec,
                  _resident((1, DA_ROW))],
        out_specs=row,
        out_shape=jax.ShapeDtypeStruct((B, L, D_MODEL), BF16),
        scratch_shapes=[pltpu.VMEM((P * DA_HEADS, DA_ROW), BF16)] * 2,
        compiler_params=_params(1),
        name="attn_sample",
    )(q, k_stack, v_stack, cache_k, cache_v, *lam_p, nw)


def _gla_kernel(q_ref, k_ref, v_ref, g_ref, r_ref, s0_ref, nw_ref, o_ref, sout_ref,
                qs_ref, ks_ref, vs_ref, os_ref, S_ref, *, C, nchunk):
    A = C // SUBLANES
    i = pl.program_id(2)

    @pl.when(i == 0)
    def _():
        S_ref[...] = s0_ref[...]

    qs_ref[...] = q_ref[...].astype(F32)
    ks_ref[...] = k_ref[...].astype(F32)
    for half in range(GLA_DV // LANES):
        vs_ref[half] = v_ref[:, half * LANES:(half + 1) * LANES].astype(F32)

    def time_of(r):
        return (r % SUBLANES) * A + r // SUBLANES

    t_r = time_of(lax.broadcasted_iota(jnp.int32, (C, C), 0))
    t_c = time_of(lax.broadcasted_iota(jnp.int32, (C, C), 1))
    t_k = time_of(lax.broadcasted_iota(jnp.int32, (C, GLA_DK), 0))
    halves = []
    hsz = C // 2
    while hsz >= A:
        halves.append(hsz)
        hsz //= 2
    sel = [t_c <= t_r] + [t_c <= (t_r // (2 * h)) * (2 * h) + h - 1 for h in halves]
    sel = jnp.concatenate([jnp.where(m, 1.0, 0.0) for m in sel], axis=0).astype(BF16)
    same_parent = [(t_r // (2 * h)) == (t_c // (2 * h)) for h in halves]
    second_half = [(t_k // h) % 2 == 1 for h in halves]
    r8 = lax.broadcasted_iota(jnp.int32, (SUBLANES, C), 0)
    c8 = lax.broadcasted_iota(jnp.int32, (SUBLANES, C), 1)
    ones_cv = jnp.ones((C, GLA_DV), BF16)

    def load(ref, base):
        return jnp.concatenate(
            [ref[pl.ds(base + a, SUBLANES, stride=A), :] for a in range(A)], axis=0)

    for c in range(nchunk):
        base = c * C
        q, k, g = load(qs_ref, base), load(ks_ref, base), load(g_ref, base)
        v = jnp.concatenate([load(vs_ref.at[half], base) for half in range(GLA_DV // LANES)],
                            axis=1).astype(BF16)
        g_hi = g.astype(BF16)
        g_lo = (g - g_hi.astype(F32)).astype(BF16)
        sums = jnp.dot(sel, jnp.concatenate([g_hi, g_lo], axis=1), preferred_element_type=F32)
        sums = sums[:, :GLA_DK] + sums[:, GLA_DK:]
        b = sums[:C]
        S = S_ref[...]
        o = jnp.dot((q * jnp.exp(b)).astype(BF16), S.astype(BF16), preferred_element_type=F32)

        amat = jnp.zeros((C, C), F32)
        for lvl, h in enumerate(halves):
            x = b - sums[(lvl + 1) * C:(lvl + 2) * C]
            sec = second_half[lvl]
            qh = jnp.where(sec, q * jnp.exp(jnp.where(sec, x, 0.0)), 0.0).astype(BF16)
            kh = jnp.where(sec, 0.0, k * jnp.exp(jnp.where(sec, 0.0, -x))).astype(BF16)
            al = lax.dot_general(qh, kh, _NT, preferred_element_type=F32)
            amat = amat + jnp.where(same_parent[lvl], al, 0.0)
        diag = []
        for a in range(A):
            ra = slice(a * SUBLANES, (a + 1) * SUBLANES)
            acc = jnp.zeros((SUBLANES, C), F32)
            for a2 in range(a + 1):
                rb = slice(a2 * SUBLANES, (a2 + 1) * SUBLANES)
                e = q[ra] * k[rb] * jnp.exp(b[ra] - b[rb])
                acc = acc + jnp.where(c8 == a2 * SUBLANES + r8,
                                      jnp.sum(e, axis=1, keepdims=True), 0.0)
            diag.append(acc)
        amat = amat + jnp.concatenate(diag, axis=0)
        o = o + jnp.dot(amat.astype(BF16), v, preferred_element_type=F32)

        kdec = (k * jnp.exp(b[C - 1:C] - b)).astype(BF16)
        upd = lax.dot_general(kdec, v, _TN, preferred_element_type=F32)
        decay = (lax.dot_general(g_hi, ones_cv, _TN, preferred_element_type=F32)
                 + lax.dot_general(g_lo, ones_cv, _TN, preferred_element_type=F32))
        S_ref[...] = jnp.exp(decay) * S + upd
        for a in range(A):
            for half in range(GLA_DV // LANES):
                os_ref[half, pl.ds(base + a, SUBLANES, stride=A), :] = (
                    o[a * SUBLANES:(a + 1) * SUBLANES, half * LANES:(half + 1) * LANES])

    og = jnp.concatenate([os_ref[half] for half in range(GLA_DV // LANES)], axis=1)
    y = _rms(og, nw_ref[...])
    r = r_ref[...].astype(F32)
    o_ref[...] = (y * (r * jax.nn.sigmoid(r))).astype(BF16)

    @pl.when(i == pl.num_programs(2) - 1)
    def _():
        sout_ref[...] = S_ref[...]


def _gla(qg, kg, vg, gk, rg, s0, nw):
    B, L, _ = qg.shape
    C = CHUNK if L % CHUNK == 0 else L
    tc = min(256, L)
    dk = lambda b, h, i: (b, i, h)
    st = pl.BlockSpec((None, None, GLA_DK, GLA_DV), lambda b, h, i: (b, h, 0, 0))
    return pl.pallas_call(
        functools.partial(_gla_kernel, C=C, nchunk=tc // C),
        grid=(B, GLA_HEADS, L // tc),
        in_specs=[pl.BlockSpec((None, tc, GLA_DK), dk), pl.BlockSpec((None, tc, GLA_DK), dk),
                  pl.BlockSpec((None, tc, GLA_DV), dk), pl.BlockSpec((None, tc, GLA_DK), dk),
                  pl.BlockSpec((None, tc, GLA_DV), dk), st, _resident((1, GLA_DV))],
        out_specs=[pl.BlockSpec((None, tc, GLA_DV), dk), st],
        out_shape=[jax.ShapeDtypeStruct((B, L, D_MODEL), BF16),
                   jax.ShapeDtypeStruct((B, GLA_HEADS, GLA_DK, GLA_DV), F32)],
        scratch_shapes=[pltpu.VMEM((tc, GLA_DK), F32), pltpu.VMEM((tc, GLA_DK), F32),
                        pltpu.VMEM((GLA_DV // LANES, tc, LANES), F32),
                        pltpu.VMEM((GLA_DV // LANES, tc, LANES), F32),
                        pltpu.VMEM((GLA_DK, GLA_DV), F32)],
        compiler_params=_params(3),
        name="gla",
    )(qg, kg, vg, gk, rg, s0, nw)


def _merge_kernel(oa_ref, ob_ref, ga_ref, gb_ref, x_ref, wo_ref, nw_ref, h_ref):
    f = lambda r: r[...].astype(F32)
    merged = jax.nn.sigmoid(f(ga_ref)) * f(oa_ref) + jax.nn.sigmoid(f(gb_ref)) * f(ob_ref)
    y = jnp.dot(merged.astype(BF16), wo_ref[...], preferred_element_type=F32)
    h_ref[...] = x_ref[...] + _rms(y, nw_ref[...])


def _merge(oa, ob, ga, gb, x2d, wo, nw):
    T = x2d.shape[0]
    tm = min(512, T)
    row = pl.BlockSpec((tm, D_MODEL), lambda i: (i, 0))
    return pl.pallas_call(
        _merge_kernel,
        grid=(T // tm,),
        in_specs=[row, row, row, row, row, _resident((D_MODEL, D_MODEL)), _resident((1, D_MODEL))],
        out_specs=row,
        out_shape=jax.ShapeDtypeStruct((T, D_MODEL), F32),
        compiler_params=_params(1),
        name="merge",
    )(oa, ob, ga, gb, x2d, wo, nw)


_FF_BLK = 256


def _ffn_kernel(h_ref, nw1_ref, wup_ref, cw_ref, cb_ref, wdn_ref, past_ref, nw2_ref,
                out_ref, cs_ref, act_ref, carry_ref, *, tiles_per_seq):
    i = pl.program_id(0)
    tm = h_ref.shape[0]

    @pl.when(i % tiles_per_seq == 0)
    def _():
        carry_ref[SUBLANES - 2:SUBLANES, :] = past_ref[...]

    h = h_ref[...]
    hb = _rms(h, nw1_ref[...]).astype(BF16)
    rowid = lax.broadcasted_iota(jnp.int32, (tm, _FF_BLK), 0)
    for blk in range(D_FF // _FF_BLK):
        cols = slice(blk * _FF_BLK, (blk + 1) * _FF_BLK)
        u = jnp.dot(hb, wup_ref[:, cols], preferred_element_type=F32)
        g = jnp.dot(hb, wup_ref[:, D_FF + blk * _FF_BLK:D_FF + (blk + 1) * _FF_BLK],
                    preferred_element_type=F32)
        prev2 = carry_ref[SUBLANES - 2:SUBLANES - 1, cols]
        prev1 = carry_ref[SUBLANES - 1:SUBLANES, cols]
        g1 = jnp.where(rowid == 0, prev1, pltpu.roll(g, 1, 0))
        g2 = jnp.where(rowid == 0, prev2, jnp.where(rowid == 1, prev1, pltpu.roll(g, 2, 0)))
        gc = cb_ref[:, cols] + cw_ref[0:1, cols] * g2 + cw_ref[1:2, cols] * g1 + cw_ref[2:3, cols] * g
        act_ref[:, cols] = (jax.nn.gelu(gc, approximate=True) * u).astype(BF16)
        carry_ref[:, cols] = g[tm - SUBLANES:tm, :]
        cs_ref[:, cols] = g[tm - (CONV_W - 1):tm, :]
    ffn = jnp.dot(act_ref[...], wdn_ref[...], preferred_element_type=F32)
    out_ref[...] = h + _rms(ffn, nw2_ref[...])


def _ffn(h2d, seq_len, nw1, wup, cw, cb, wdn, conv_past, nw2):
    T = h2d.shape[0]
    nseq = T // seq_len
    tm = min(512, seq_len)
    tps = seq_len // tm
    row = pl.BlockSpec((tm, D_MODEL), lambda i: (i, 0))
    st = pl.BlockSpec((None, CONV_W - 1, D_FF), lambda i: (i // tps, 0, 0))
    return pl.pallas_call(
        functools.partial(_ffn_kernel, tiles_per_seq=tps),
        grid=(T // tm,),
        in_specs=[row, _resident((1, D_MODEL)), _resident((D_MODEL, 2 * D_FF)),
                  _resident((CONV_W, D_FF)), _resident((1, D_FF)), _resident((D_FF, D_MODEL)),
                  st, _resident((1, D_MODEL))],
        out_specs=[row, st],
        out_shape=[jax.ShapeDtypeStruct((T, D_MODEL), F32),
                   jax.ShapeDtypeStruct((nseq, CONV_W - 1, D_FF), F32)],
        scratch_shapes=[pltpu.VMEM((tm, D_FF), BF16), pltpu.VMEM((SUBLANES, D_FF), F32)],
        compiler_params=_params(1),
        name="ffn",
    )(h2d, nw1, wup, cw, cb, wdn, conv_past, nw2)


def _layer(x2d, nseq, seq_len, pos0, caches, li, depth, kv_stack, s0, conv_past, p):
    (q, k, v, kb, vb, qg, kg, vg, rg, gk, ga, gb) = _inproj(
        x2d, seq_len, pos0, p["pre_mix_w"], p["w_cat"], p["w_gk2"], p["b_gk2"], li, depth,
        kv_stack)
    lam_init = 0.8 - 0.6 * math.exp(-0.3 * li)
    lam_p = (p["lq1"], p["lk1"], p["lq2"], p["lk2"])
    sq = lambda t: t.reshape(nseq, seq_len, t.shape[-1])
    if caches is None:
        oa = _attn_prompt(sq(q), sq(kb), sq(vb), lam_p, p["da_norm_w"], lam_init)
    else:
        oa = _attn_sample(sq(q), k, v, caches[0], caches[1], li, lam_p, p["da_norm_w"], lam_init)
    ob, s_new = _gla(sq(qg), sq(kg), sq(vg), sq(gk), sq(rg), s0, p["gla_norm_w"])
    T = x2d.shape[0]
    h = _merge(oa.reshape(T, D_MODEL), ob.reshape(T, D_MODEL), ga, gb, x2d, p["w_o"],
               p["post_mix_w"])
    out, conv_new = _ffn(h, seq_len, p["pre_ffn_w"], p["w_up"], p["conv_w"], p["conv_b"],
                         p["w_down"], conv_past, p["post_ffn_w"])
    return out, (k, v), s_new, conv_new


def kernel(x_prompt, x_sample, cache_k, cache_v, state_gla, state_conv, w_in, w_gk2, b_gk2, lambda_q1, lambda_k1, lambda_q2, lambda_k2, da_norm_w, gla_norm_w, w_o, pre_mix_w, post_mix_w, pre_ffn_w, post_ffn_w, w_up, conv_w, conv_b, w_down):
    B, S, _ = x_prompt.shape
    Bs, L, _ = x_sample.shape
    depth, _, P = cache_k.shape[:3]
    hp = x_prompt.reshape(B * S, D_MODEL)
    hs = x_sample.reshape(Bs * L, D_MODEL)
    s0_p = jnp.zeros((B, GLA_HEADS, GLA_DK, GLA_DV), F32)
    conv0_p = jnp.zeros((B, CONV_W - 1, D_FF), F32)
    c_lr = _C_LR - 2 * D_MODEL + GLA_RANK
    outs_p, outs_s = [], []
    kv_p = kv_s = None
    for li in range(depth):
        w = w_in[li]
        w_cat = jnp.concatenate(
            [w[:, :c_lr - GLA_RANK], w[:, c_lr:], w[:, c_lr - GLA_RANK:c_lr],
             jnp.zeros((D_MODEL, LANES - GLA_RANK), w.dtype)], axis=1).astype(BF16)
        p = dict(
            w_cat=w_cat,
            w_gk2=jnp.concatenate([w_gk2[li], jnp.zeros((LANES - GLA_RANK, w_gk2.shape[-1]),
                                                        w_gk2.dtype)], axis=0).astype(BF16),
            b_gk2=b_gk2[li][None, :],
            lq1=lambda_q1[li][None, :], lk1=lambda_k1[li][None, :],
            lq2=lambda_q2[li][None, :], lk2=lambda_k2[li][None, :],
            da_norm_w=da_norm_w[li][None, :], gla_norm_w=gla_norm_w[li][None, :],
            w_o=w_o[li].astype(BF16),
            pre_mix_w=pre_mix_w[li][None, :], post_mix_w=post_mix_w[li][None, :],
            pre_ffn_w=pre_ffn_w[li][None, :], post_ffn_w=post_ffn_w[li][None, :],
            w_up=w_up[li].astype(BF16), conv_w=conv_w[li], conv_b=conv_b[li][None, :],
            w_down=w_down[li].astype(BF16))
        hp, kv_p, *rest_p = _layer(hp, B, S, 0, None, li, depth, kv_p, s0_p, conv0_p, p)
        hs, kv_s, *rest_s = _layer(hs, Bs, L, P, (cache_k, cache_v), li, depth, kv_s,
                                   state_gla[li], state_conv[li], p)
        outs_p.append(rest_p)
        outs_s.append(rest_s)
    stack = lambda outs, j: jnp.stack([o[j] for o in outs])
    kv5 = lambda t, n, l: t.reshape(depth, n, l, DA_HEADS, DA_ROW)
    return (hp.reshape(B, S, D_MODEL), hs.reshape(Bs, L, D_MODEL),
            kv5(kv_p[0], B, S), kv5(kv_p[1], B, S), stack(outs_p, 0), stack(outs_p, 1),
            kv5(kv_s[0], Bs, L), kv5(kv_s[1], Bs, L), stack(outs_s, 0), stack(outs_s, 1))
```

```python
import functools
import math

import jax
import jax.numpy as jnp
from jax import lax
from jax.experimental import pallas as pl
from jax.experimental.pallas import tpu as pltpu

F32 = jnp.float32
BF16 = jnp.bfloat16

D_MODEL = 1024
CHUNK = 64
DA_HEADS = 8
DA_HEAD_DIM = 64
DA_ROW = 2 * DA_HEAD_DIM
ROPE_DIM = DA_HEAD_DIM // 4
ROPE_HALF = ROPE_DIM // 2
ROPE_THETA = 500000.0
GLA_HEADS = 4
GLA_DK = 128
GLA_DV = 256
GLA_RANK = 16
GLA_GATE_NORM = 16.0
D_FF = 2816
CONV_W = 3
EPS = 1e-6

LANES = 128
SUBLANES = 8
VMEM_LIMIT = 56 * 1024 * 1024

_C_QA, _C_KA, _C_VA = 0, 1024, 2048
_C_QG, _C_KG, _C_VG, _C_RG = 3072, 3584, 4096, 5120
_C_GA, _C_GB, _C_LR = 6144, 7168, 8192
_W_COLS = _C_LR + LANES

_Q_SCALE = DA_HEAD_DIM ** -0.5 * math.log2(math.e)

_NT = (((1,), (1,)), ((), ()))
_TN = (((0,), (0,)), ((), ()))


def _rms(x, w):
    return x * lax.rsqrt(jnp.mean(x * x, axis=-1, keepdims=True) + EPS) * w


def _params(n_axes):
    return pltpu.CompilerParams(
        dimension_semantics=("arbitrary",) * n_axes, vmem_limit_bytes=VMEM_LIMIT)


def _resident(shape):
    nd = len(shape)
    return pl.BlockSpec(shape, lambda *_: (0,) * nd, pipeline_mode=pl.Buffered(1))


def _inproj_kernel(x_ref, nw_ref, w_ref, cos_ref, sna_ref, snb_ref, wgk_ref, bgk_ref, *refs):
    (q_ref, k_ref, v_ref, vt_ref, kb_ref, qg_ref, kg_ref, vg_ref, rg_ref, gk_ref,
     ga_ref, gb_ref) = refs[-12:]
    xb = _rms(x_ref[...], nw_ref[...]).astype(BF16)

    def proj(c0, n):
        return jnp.dot(xb, w_ref[:, c0:c0 + n], preferred_element_type=F32)

    cos, sna, snb = cos_ref[...], sna_ref[...], snb_ref[...]

    def rope(y):
        return (y * cos + pltpu.roll(y, LANES - ROPE_HALF, 1) * sna
                + pltpu.roll(y, ROPE_HALF, 1) * snb)

    for blk in range(2):
        yq = proj(_C_QA + blk * 512, 512)
        yk = proj(_C_KA + blk * 512, 512)
        for h in range(4):
            cols = slice(blk * 512 + h * DA_ROW, blk * 512 + (h + 1) * DA_ROW)
            q_ref[:, cols] = (rope(yq[:, h * DA_ROW:(h + 1) * DA_ROW]) * _Q_SCALE).astype(BF16)
            kr = rope(yk[:, h * DA_ROW:(h + 1) * DA_ROW])
            k_ref[:, blk * 4 + h, :] = kr
            kb_ref[:, cols] = kr.astype(BF16)
        yv = proj(_C_VA + blk * 512, 512)
        for h in range(4):
            v_ref[:, blk * 4 + h, :] = yv[:, h * DA_ROW:(h + 1) * DA_ROW]
        vt_ref[blk * 512:(blk + 1) * 512, :] = yv.T.astype(BF16)
    qg_ref[...] = (proj(_C_QG, 512) * (GLA_DK ** -0.5)).astype(BF16)
    kg_ref[...] = proj(_C_KG, 512).astype(BF16)
    for c0, dst in ((_C_VG, vg_ref), (_C_RG, rg_ref), (_C_GA, ga_ref), (_C_GB, gb_ref)):
        for blk in range(2):
            dst[:, blk * 512:(blk + 1) * 512] = proj(c0 + blk * 512, 512).astype(BF16)
    lr = proj(_C_LR, LANES)
    z = jnp.dot(lr.astype(BF16), wgk_ref[...], preferred_element_type=F32) + bgk_ref[...]
    gk_ref[...] = (jnp.minimum(z, 0.0) - jnp.log1p(jnp.exp(-jnp.abs(z)))) * (1.0 / GLA_GATE_NORM)


def _rope_tables(pos):
    inv = ROPE_THETA ** (-jnp.arange(ROPE_HALF, dtype=F32) * 2.0 / ROPE_DIM)
    ang = pos.astype(F32)[:, None] * inv[None, :]
    cos, sin = jnp.cos(ang), jnp.sin(ang)
    n = pos.shape[0]
    rest = DA_HEAD_DIM - ROPE_DIM
    z8 = jnp.zeros((n, ROPE_HALF), F32)
    cos64 = jnp.concatenate([cos, cos, jnp.ones((n, rest), F32)], axis=-1)
    sna64 = jnp.concatenate([-sin, z8, jnp.zeros((n, rest), F32)], axis=-1)
    snb64 = jnp.concatenate([z8, sin, jnp.zeros((n, rest), F32)], axis=-1)
    return tuple(jnp.tile(t, (1, 2)) for t in (cos64, sna64, snb64))


def _inproj(x2d, seq_len, pos0, nw, w_cat, wgk, bgk, li, depth, kv_stack):
    T = x2d.shape[0]
    tm = min(256, T)
    period = max(seq_len, tm)
    pos = pos0 + jnp.arange(period, dtype=jnp.int32) % seq_len
    cos, sna, snb = _rope_tables(pos)
    nper = period // tm
    row = lambda w: pl.BlockSpec((tm, w), lambda i: (i, 0))
    tab = pl.BlockSpec((tm, LANES), lambda i: (i % nper, 0))
    stack = pl.BlockSpec((None, tm, DA_HEADS, DA_ROW), lambda i: (li, i, 0, 0))
    stack_shape = jax.ShapeDtypeStruct((depth, T, DA_HEADS, DA_ROW), F32)
    vt = pl.BlockSpec((None, D_MODEL, tm), lambda i: (i, 0, 0))
    vt_shape = jax.ShapeDtypeStruct((T // tm, D_MODEL, tm), BF16)
    widths = (1024, 1024, 512, 512, 1024, 1024, 512, 1024, 1024)
    dtypes = (BF16, BF16, BF16, BF16, BF16, BF16, F32, BF16, BF16)
    rows = [row(w) for w in widths]
    shapes = [jax.ShapeDtypeStruct((T, w), d) for w, d in zip(widths, dtypes)]
    prev = () if kv_stack is None else tuple(kv_stack)
    n_in = 8
    return pl.pallas_call(
        _inproj_kernel,
        grid=(T // tm,),
        in_specs=[row(D_MODEL), _resident((1, D_MODEL)), _resident((D_MODEL, _W_COLS)),
                  tab, tab, tab, _resident((LANES, 512)), _resident((1, 512))]
                 + [pl.BlockSpec(memory_space=pl.ANY)] * len(prev),
        out_specs=rows[:1] + [stack, stack, vt] + rows[1:],
        out_shape=shapes[:1] + [stack_shape, stack_shape, vt_shape] + shapes[1:],
        input_output_aliases={n_in + j: 1 + j for j in range(len(prev))},
        compiler_params=_params(1),
        name="inproj",
    )(x2d, nw, w_cat, cos, sna, snb, wgk, bgk, *prev)


def _split_q(q):
    lane = lax.broadcasted_iota(jnp.int32, q.shape, 1)
    zero = jnp.zeros_like(q)
    return jnp.concatenate([jnp.where(lane < DA_HEAD_DIM, q, zero),
                            jnp.where(lane < DA_HEAD_DIM, zero, q)], axis=0)


def _lambda(lq1, lk1, lq2, lk2, lam_init):
    return (jnp.exp(jnp.sum(lq1[...] * lk1[...], axis=1, keepdims=True))
            - jnp.exp(jnp.sum(lq2[...] * lk2[...], axis=1, keepdims=True)) + lam_init)


def _attn_finish(o12, lam, nw, lam_init, n):
    od = o12[:n] - lam * o12[n:]
    return (_rms(od, nw) * (1.0 - lam_init)).astype(BF16)


_ATTN_HEADS_PER_STEP = 4


_ONES_ROWS = 16


def _attn_prompt_kernel(q_ref, k_ref, vt_ref, lq1, lk1, lq2, lk2, nwc_ref, o_ref,
                        q2_ref, sa_ref, sb_ref, m_ref, l_ref, acc_ref, *, tq, nh, lam_init):
    i = pl.program_id(2)
    heads = [slice(h * DA_ROW, (h + 1) * DA_ROW) for h in range(nh)]
    for h, cols in enumerate(heads):
        q2_ref[h] = _split_q(q_ref[:, cols])
    m_ref[...] = jnp.full(m_ref.shape, -jnp.inf, F32)
    l_ref[...] = jnp.zeros(l_ref.shape, F32)
    acc_ref[...] = jnp.zeros(acc_ref.shape, F32)
    ones = jnp.ones((_ONES_ROWS, tq), BF16)

    def scores(j, dst):
        r = pl.multiple_of(j * tq, tq)
        for h, cols in enumerate(heads):
            dst[h] = lax.dot_general(k_ref[pl.ds(r, tq), cols], q2_ref[h], _NT,
                                     preferred_element_type=F32)

    def consume(j, src, masked):
        for h, cols in enumerate(heads):
            s = src[h]
            if masked:
                key = lax.broadcasted_iota(jnp.int32, s.shape, 0)
                qry = lax.broadcasted_iota(jnp.int32, s.shape, 1)
                s = jnp.where((key // CHUNK) <= ((qry % tq) // CHUNK), s, -jnp.inf)
            m_prev = m_ref[h]
            m_new = jnp.maximum(m_prev, jnp.max(s, axis=0, keepdims=True))
            alpha = jnp.exp2(m_prev - m_new)
            p = jnp.exp2(s - m_new)
            vte = jnp.concatenate([vt_ref[j, cols, :], ones], axis=0)
            pv = jnp.dot(vte, p.astype(BF16), preferred_element_type=F32)
            acc_ref[h] = alpha * acc_ref[h] + pv[:DA_ROW]
            l_ref[h] = alpha * l_ref[h] + pv[DA_ROW:DA_ROW + 1]
            m_ref[h] = m_new

    scores(0, sa_ref)

    def pair(t, carry):
        j = 2 * t
        scores(j + 1, sb_ref)
        consume(j, sa_ref, False)
        scores(j + 2, sa_ref)
        consume(j + 1, sb_ref, False)
        return carry
    lax.fori_loop(0, lax.shift_right_logical(i, 1), pair, 0)

    @pl.when(lax.bitwise_and(i, 1) == 0)
    def _():
        consume(i, sa_ref, True)

    @pl.when(lax.bitwise_and(i, 1) == 1)
    def _():
        scores(i, sb_ref)
        consume(i - 1, sa_ref, False)
        consume(i, sb_ref, True)

    lam = _lambda(lq1, lk1, lq2, lk2, lam_init)
    for h in range(nh):
        o12 = acc_ref[h] / l_ref[h]
        od = o12[:, :tq] - lam * o12[:, tq:]
        y = od * lax.rsqrt(jnp.mean(od * od, axis=0, keepdims=True) + EPS) * nwc_ref[...]
        o_ref[:, h * DA_ROW:(h + 1) * DA_ROW] = (y * (1.0 - lam_init)).T.astype(BF16)


def _attn_prompt(q, k, vt, lam_p, nw, lam_init):
    B, S, _ = q.shape
    tq = vt.shape[-1]
    nh = _ATTN_HEADS_PER_STEP
    w = nh * DA_ROW
    lam_spec = _resident((1, DA_HEAD_DIM))
    stat = pltpu.VMEM((nh, 1, 2 * tq), F32)
    return pl.pallas_call(
        functools.partial(_attn_prompt_kernel, tq=tq, nh=nh, lam_init=lam_init),
        grid=(B, DA_HEADS // nh, S // tq),
        in_specs=[pl.BlockSpec((None, tq, w), lambda b, h, i: (b, i, h)),
                  pl.BlockSpec((None, S, w), lambda b, h, i: (b, 0, h)),
                  pl.BlockSpec((None, S // tq, w, tq), lambda b, h, i: (b, 0, h, 0)),
                  lam_spec, lam_spec, lam_spec, lam_spec, _resident((DA_ROW, 1))],
        out_specs=pl.BlockSpec((None, tq, w), lambda b, h, i: (b, i, h)),
        out_shape=jax.ShapeDtypeStruct((B, S, D_MODEL), BF16),
        scratch_shapes=[pltpu.VMEM((nh, 2 * tq, DA_ROW), BF16),
                        pltpu.VMEM((nh, tq, 2 * tq), F32), pltpu.VMEM((nh, tq, 2 * tq), F32),
                        stat, stat, pltpu.VMEM((nh, DA_ROW, 2 * tq), F32)],
        compiler_params=_params(3),
        name="attn_prompt",
    )(q, k, vt, *lam_p, nw.reshape(DA_ROW, 1))


def _attn_sample_kernel(q_ref, kn_ref, vn_ref, ck_ref, cv_ref, lq1, lk1, lq2, lk2, nw_ref, o_ref,
                        ckb_ref, cvb_ref, *, lam_init):
    L, P = q_ref.shape[0], ck_ref.shape[0]
    flat = lambda r: r[...].reshape(r.shape[0] * DA_HEADS, DA_ROW).astype(BF16)
    ckb_ref[...] = flat(ck_ref)
    cvb_ref[...] = flat(cv_ref)
    kn, vn = flat(kn_ref), flat(vn_ref)
    head_p = lax.broadcasted_iota(jnp.int32, (2 * L, P * DA_HEADS), 1) % DA_HEADS
    head_n = lax.broadcasted_iota(jnp.int32, (2 * L, L * DA_HEADS), 1) % DA_HEADS
    lam = _lambda(lq1, lk1, lq2, lk2, lam_init)
    for h in range(DA_HEADS):
        cols = slice(h * DA_ROW, (h + 1) * DA_ROW)
        q2 = _split_q(q_ref[:, cols])
        s_p = lax.dot_general(q2, ckb_ref[...], _NT, preferred_element_type=F32)
        s_n = lax.dot_general(q2, kn, _NT, preferred_element_type=F32)
        s_p = jnp.where(head_p == h, s_p, -jnp.inf)
        s_n = jnp.where(head_n == h, s_n, -jnp.inf)
        m = jnp.maximum(jnp.max(s_p, axis=1, keepdims=True), jnp.max(s_n, axis=1, keepdims=True))
        p_p = jnp.exp2(s_p - m)
        p_n = jnp.exp2(s_n - m)
        l = jnp.sum(p_p, axis=1, keepdims=True) + jnp.sum(p_n, axis=1, keepdims=True)
        acc = (jnp.dot(p_p.astype(BF16), cvb_ref[...], preferred_element_type=F32)
               + jnp.dot(p_n.astype(BF16), vn, preferred_element_type=F32))
        o_ref[:, cols] = _attn_finish(acc / l, lam, nw_ref[...], lam_init, L)


def _attn_sample(q, k_stack, v_stack, cache_k, cache_v, li, lam_p, nw, lam_init):
    B, L, _ = q.shape
    P = cache_k.shape[2]
    row = pl.BlockSpec((None, L, D_MODEL), lambda b: (b, 0, 0))
    new = pl.BlockSpec((None, L, DA_HEADS, DA_ROW), lambda b: (li, b, 0, 0))
    past = pl.BlockSpec((None, None, P, DA_HEADS, DA_ROW), lambda b: (li, b, 0, 0, 0))
    lam_spec = _resident((1, DA_HEAD_DIM))
    return pl.pallas_call(
        functools.partial(_attn_sample_kernel, lam_init=lam_init),
        grid=(B,),
        in_specs=[row, new, new, past, past, lam_spec, lam_spec, lam_spec, lam_spec,
                  _resident((1, DA_ROW))],
        out_specs=row,
        out_shape=jax.ShapeDtypeStruct((B, L, D_MODEL), BF16),
        scratch_shapes=[pltpu.VMEM((P * DA_HEADS, DA_ROW), BF16)] * 2,
        compiler_params=_params(1),
        name="attn_sample",
    )(q, k_stack, v_stack, cache_k, cache_v, *lam_p, nw)


def _gla_kernel(q_ref, k_ref, v_ref, g_ref, r_ref, s0_ref, nw_ref, o_ref, sout_ref,
                qs_ref, ks_ref, vs_ref, os_ref, S_ref, *, C, nchunk):
    A = C // SUBLANES
    i = pl.program_id(2)

    @pl.when(i == 0)
    def _():
        S_ref[...] = s0_ref[...]

    qs_ref[...] = q_ref[...].astype(F32)
    ks_ref[...] = k_ref[...].astype(F32)
    for half in range(GLA_DV // LANES):
        vs_ref[half] = v_ref[:, half * LANES:(half + 1) * LANES].astype(F32)

    def time_of(r):
        return (r % SUBLANES) * A + r // SUBLANES

    t_r = time_of(lax.broadcasted_iota(jnp.int32, (C, C), 0))
    t_c = time_of(lax.broadcasted_iota(jnp.int32, (C, C), 1))
    t_k = time_of(lax.broadcasted_iota(jnp.int32, (C, GLA_DK), 0))
    halves = []
    hsz = C // 2
    while hsz >= A:
        halves.append(hsz)
        hsz //= 2
    sel = [t_c <= t_r] + [t_c <= (t_r // (2 * h)) * (2 * h) + h - 1 for h in halves]
    sel = jnp.concatenate([jnp.where(m, 1.0, 0.0) for m in sel], axis=0).astype(BF16)
    same_parent = [(t_r // (2 * h)) == (t_c // (2 * h)) for h in halves]
    second_half = [(t_k // h) % 2 == 1 for h in halves]
    r8 = lax.broadcasted_iota(jnp.int32, (SUBLANES, C), 0)
    c8 = lax.broadcasted_iota(jnp.int32, (SUBLANES, C), 1)
    ones_cv = jnp.ones((C, GLA_DV), BF16)

    def load(ref, base):
        return jnp.concatenate(
            [ref[pl.ds(base + a, SUBLANES, stride=A), :] for a in range(A)], axis=0)

    for c in range(nchunk):
        base = c * C
        q, k, g = load(qs_ref, base), load(ks_ref, base), load(g_ref, base)
        v = jnp.concatenate([load(vs_ref.at[half], base) for half in range(GLA_DV // LANES)],
                            axis=1).astype(BF16)
        g_hi = g.astype(BF16)
        g_lo = (g - g_hi.astype(F32)).astype(BF16)
        sums = jnp.dot(sel, jnp.concatenate([g_hi, g_lo], axis=1), preferred_element_type=F32)
        sums = sums[:, :GLA_DK] + sums[:, GLA_DK:]
        b = sums[:C]
        S = S_ref[...]
        o = jnp.dot((q * jnp.exp(b)).astype(BF16), S.astype(BF16), preferred_element_type=F32)

        amat = jnp.zeros((C, C), F32)
        for lvl, h in enumerate(halves):
            x = b - sums[(lvl + 1) * C:(lvl + 2) * C]
            sec = second_half[lvl]
            qh = jnp.where(sec, q * jnp.exp(jnp.where(sec, x, 0.0)), 0.0).astype(BF16)
            kh = jnp.where(sec, 0.0, k * jnp.exp(jnp.where(sec, 0.0, -x))).astype(BF16)
            al = lax.dot_general(qh, kh, _NT, preferred_element_type=F32)
            amat = amat + jnp.where(same_parent[lvl], al, 0.0)
        diag = []
        for a in range(A):
            ra = slice(a * SUBLANES, (a + 1) * SUBLANES)
            acc = jnp.zeros((SUBLANES, C), F32)
            for a2 in range(a + 1):
                rb = slice(a2 * SUBLANES, (a2 + 1) * SUBLANES)
                e = q[ra] * k[rb] * jnp.exp(b[ra] - b[rb])
                acc = acc + jnp.where(c8 == a2 * SUBLANES + r8,
                                      jnp.sum(e, axis=1, keepdims=True), 0.0)
            diag.append(acc)
        amat = amat + jnp.concatenate(diag, axis=0)
        o = o + jnp.dot(amat.astype(BF16), v, preferred_element_type=F32)

        kdec = (k * jnp.exp(b[C - 1:C] - b)).astype(BF16)
        upd = lax.dot_general(kdec, v, _TN, preferred_element_type=F32)
        decay = (lax.dot_general(g_hi, ones_cv, _TN, preferred_element_type=F32)
                 + lax.dot_general(g_lo, ones_cv, _TN, preferred_element_type=F32))
        S_ref[...] = jnp.exp(decay) * S + upd
        for a in range(A):
            for half in range(GLA_DV // LANES):
                os_ref[half, pl.ds(base + a, SUBLANES, stride=A), :] = (
                    o[a * SUBLANES:(a + 1) * SUBLANES, half * LANES:(half + 1) * LANES])

    og = jnp.concatenate([os_ref[half] for half in range(GLA_DV // LANES)], axis=1)
    y = _rms(og, nw_ref[...])
    r = r_ref[...].astype(F32)
    o_ref[...] = (y * (r * jax.nn.sigmoid(r))).astype(BF16)

    @pl.when(i == pl.num_programs(2) - 1)
    def _():
        sout_ref[...] = S_ref[...]


def _gla(qg, kg, vg, gk, rg, s0, nw):
    B, L, _ = qg.shape
    C = CHUNK if L % CHUNK == 0 else L
    tc = min(256, L)
    dk = lambda b, h, i: (b, i, h)
    st = pl.BlockSpec((None, None, GLA_DK, GLA_DV), lambda b, h, i: (b, h, 0, 0))
    return pl.pallas_call(
        functools.partial(_gla_kernel, C=C, nchunk=tc // C),
        grid=(B, GLA_HEADS, L // tc),
        in_specs=[pl.BlockSpec((None, tc, GLA_DK), dk), pl.BlockSpec((None, tc, GLA_DK), dk),
                  pl.BlockSpec((None, tc, GLA_DV), dk), pl.BlockSpec((None, tc, GLA_DK), dk),
                  pl.BlockSpec((None, tc, GLA_DV), dk), st, _resident((1, GLA_DV))],
        out_specs=[pl.BlockSpec((None, tc, GLA_DV), dk), st],
        out_shape=[jax.ShapeDtypeStruct((B, L, D_MODEL), BF16),
                   jax.ShapeDtypeStruct((B, GLA_HEADS, GLA_DK, GLA_DV), F32)],
        scratch_shapes=[pltpu.VMEM((tc, GLA_DK), F32), pltpu.VMEM((tc, GLA_DK), F32),
                        pltpu.VMEM((GLA_DV // LANES, tc, LANES), F32),
                        pltpu.VMEM((GLA_DV // LANES, tc, LANES), F32),
                        pltpu.VMEM((GLA_DK, GLA_DV), F32)],
        compiler_params=_params(3),
        name="gla",
    )(qg, kg, vg, gk, rg, s0, nw)


_FF_BLK = 256


def _mix_kernel(oa_ref, ob_ref, ga_ref, gb_ref, x_ref, wo_ref, nw0_ref,
                nw1_ref, wup_ref, cw_ref, cb_ref, wdn_ref, past_ref, nw2_ref,
                out_ref, cs_ref, act_ref, carry_ref, *, tiles_per_seq):
    i = pl.program_id(0)
    tm = x_ref.shape[0]

    @pl.when(i % tiles_per_seq == 0)
    def _():
        carry_ref[SUBLANES - 2:SUBLANES, :] = past_ref[...]

    f = lambda r: r[...].astype(F32)
    merged = jax.nn.sigmoid(f(ga_ref)) * f(oa_ref) + jax.nn.sigmoid(f(gb_ref)) * f(ob_ref)
    y = jnp.dot(merged.astype(BF16), wo_ref[...], preferred_element_type=F32)
    h = x_ref[...] + _rms(y, nw0_ref[...])
    hb = _rms(h, nw1_ref[...]).astype(BF16)
    rowid = lax.broadcasted_iota(jnp.int32, (tm, _FF_BLK), 0)
    for blk in range(D_FF // _FF_BLK):
        cols = slice(blk * _FF_BLK, (blk + 1) * _FF_BLK)
        u = jnp.dot(hb, wup_ref[:, cols], preferred_element_type=F32)
        g = jnp.dot(hb, wup_ref[:, D_FF + blk * _FF_BLK:D_FF + (blk + 1) * _FF_BLK],
                    preferred_element_type=F32)
        prev2 = carry_ref[SUBLANES - 2:SUBLANES - 1, cols]
        prev1 = carry_ref[SUBLANES - 1:SUBLANES, cols]
        g1 = jnp.where(rowid == 0, prev1, pltpu.roll(g, 1, 0))
        g2 = jnp.where(rowid == 0, prev2, jnp.where(rowid == 1, prev1, pltpu.roll(g, 2, 0)))
        gc = cb_ref[:, cols] + cw_ref[0:1, cols] * g2 + cw_ref[1:2, cols] * g1 + cw_ref[2:3, cols] * g
        act_ref[:, cols] = (jax.nn.gelu(gc, approximate=True) * u).astype(BF16)
        carry_ref[:, cols] = g[tm - SUBLANES:tm, :]
        cs_ref[:, cols] = g[tm - (CONV_W - 1):tm, :]
    ffn = jnp.dot(act_ref[...], wdn_ref[...], preferred_element_type=F32)
    out_ref[...] = h + _rms(ffn, nw2_ref[...])


def _mix(oa, ob, ga, gb, x2d, seq_len, wo, nw0, nw1, wup, cw, cb, wdn, conv_past, nw2):
    T = x2d.shape[0]
    nseq = T // seq_len
    tm = min(512, seq_len)
    tps = seq_len // tm
    row = pl.BlockSpec((tm, D_MODEL), lambda i: (i, 0))
    st = pl.BlockSpec((None, CONV_W - 1, D_FF), lambda i: (i // tps, 0, 0))
    vec = _resident((1, D_MODEL))
    return pl.pallas_call(
        functools.partial(_mix_kernel, tiles_per_seq=tps),
        grid=(T // tm,),
        in_specs=[row, row, row, row, row, _resident((D_MODEL, D_MODEL)), vec,
                  vec, _resident((D_MODEL, 2 * D_FF)),
                  _resident((CONV_W, D_FF)), _resident((1, D_FF)), _resident((D_FF, D_MODEL)),
                  st, vec],
        out_specs=[row, st],
        out_shape=[jax.ShapeDtypeStruct((T, D_MODEL), F32),
                   jax.ShapeDtypeStruct((nseq, CONV_W - 1, D_FF), F32)],
        scratch_shapes=[pltpu.VMEM((tm, D_FF), BF16), pltpu.VMEM((SUBLANES, D_FF), F32)],
        compiler_params=_params(1),
        name="mix",
    )(oa, ob, ga, gb, x2d, wo, nw0, nw1, wup, cw, cb, wdn, conv_past, nw2)


def _layer(x2d, nseq, seq_len, pos0, caches, li, depth, kv_stack, s0, conv_past, p):
    (q, k, v, vt, kb, qg, kg, vg, rg, gk, ga, gb) = _inproj(
        x2d, seq_len, pos0, p["pre_mix_w"], p["w_cat"], p["w_gk2"], p["b_gk2"], li, depth,
        kv_stack)
    lam_init = 0.8 - 0.6 * math.exp(-0.3 * li)
    lam_p = (p["lq1"], p["lk1"], p["lq2"], p["lk2"])
    sq = lambda t: t.reshape(nseq, seq_len, t.shape[-1])
    if caches is None:
        vt = vt.reshape(nseq, vt.shape[0] // nseq, D_MODEL, vt.shape[-1])
        oa = _attn_prompt(sq(q), sq(kb), vt, lam_p, p["da_norm_w"], lam_init)
    else:
        oa = _attn_sample(sq(q), k, v, caches[0], caches[1], li, lam_p, p["da_norm_w"], lam_init)
    ob, s_new = _gla(sq(qg), sq(kg), sq(vg), sq(gk), sq(rg), s0, p["gla_norm_w"])
    T = x2d.shape[0]
    out, conv_new = _mix(oa.reshape(T, D_MODEL), ob.reshape(T, D_MODEL), ga, gb, x2d, seq_len,
                         p["w_o"], p["post_mix_w"], p["pre_ffn_w"], p["w_up"], p["conv_w"],
                         p["conv_b"], p["w_down"], conv_past, p["post_ffn_w"])
    return out, (k, v), s_new, conv_new


def kernel(x_prompt, x_sample, cache_k, cache_v, state_gla, state_conv, w_in, w_gk2, b_gk2, lambda_q1, lambda_k1, lambda_q2, lambda_k2, da_norm_w, gla_norm_w, w_o, pre_mix_w, post_mix_w, pre_ffn_w, post_ffn_w, w_up, conv_w, conv_b, w_down):
    B, S, _ = x_prompt.shape
    Bs, L, _ = x_sample.shape
    depth, _, P = cache_k.shape[:3]
    hp = x_prompt.reshape(B * S, D_MODEL)
    hs = x_sample.reshape(Bs * L, D_MODEL)
    s0_p = jnp.zeros((B, GLA_HEADS, GLA_DK, GLA_DV), F32)
    conv0_p = jnp.zeros((B, CONV_W - 1, D_FF), F32)
    c_lr = _C_LR - 2 * D_MODEL + GLA_RANK
    outs_p, outs_s = [], []
    kv_p = kv_s = None
    for li in range(depth):
        w = w_in[li]
        w_cat = jnp.concatenate(
            [w[:, :c_lr - GLA_RANK], w[:, c_lr:], w[:, c_lr - GLA_RANK:c_lr],
             jnp.zeros((D_MODEL, LANES - GLA_RANK), w.dtype)], axis=1).astype(BF16)
        p = dict(
            w_cat=w_cat,
            w_gk2=jnp.concatenate([w_gk2[li], jnp.zeros((LANES - GLA_RANK, w_gk2.shape[-1]),
                                                        w_gk2.dtype)], axis=0).astype(BF16),
            b_gk2=b_gk2[li][None, :],
            lq1=lambda_q1[li][None, :], lk1=lambda_k1[li][None, :],
            lq2=lambda_q2[li][None, :], lk2=lambda_k2[li][None, :],
            da_norm_w=da_norm_w[li][None, :], gla_norm_w=gla_norm_w[li][None, :],
            w_o=w_o[li].astype(BF16),
            pre_mix_w=pre_mix_w[li][None, :], post_mix_w=post_mix_w[li][None, :],
            pre_ffn_w=pre_ffn_w[li][None, :], post_ffn_w=post_ffn_w[li][None, :],
            w_up=w_up[li].astype(BF16), conv_w=conv_w[li], conv_b=conv_b[li][None, :],
            w_down=w_down[li].astype(BF16))
        hp, kv_p, *rest_p = _layer(hp, B, S, 0, None, li, depth, kv_p, s0_p, conv0_p, p)
        hs, kv_s, *rest_s = _layer(hs, Bs, L, P, (cache_k, cache_v), li, depth, kv_s,
                                   state_gla[li], state_conv[li], p)
        outs_p.append(rest_p)
        outs_s.append(rest_s)
    stack = lambda outs, j: jnp.stack([o[j] for o in outs])
    kv5 = lambda t, n, l: t.reshape(depth, n, l, DA_HEADS, DA_ROW)
    return (hp.reshape(B, S, D_MODEL), hs.reshape(Bs, L, D_MODEL),
            kv5(kv_p[0], B, S), kv5(kv_p[1], B, S), stack(outs_p, 0), stack(outs_p, 1),
            kv5(kv_s[0], Bs, L), kv5(kv_s[1], Bs, L), stack(outs_s, 0), stack(outs_s, 1))
```

```python
import functools
import math

import jax
import jax.numpy as jnp
from jax import lax
from jax.experimental import pallas as pl
from jax.experimental.pallas import tpu as pltpu

F32 = jnp.float32
BF16 = jnp.bfloat16

D_MODEL = 1024
CHUNK = 64
DA_HEADS = 8
DA_HEAD_DIM = 64
DA_ROW = 2 * DA_HEAD_DIM
ROPE_DIM = DA_HEAD_DIM // 4
ROPE_HALF = ROPE_DIM // 2
ROPE_THETA = 500000.0
GLA_HEADS = 4
GLA_DK = 128
GLA_DV = 256
GLA_RANK = 16
GLA_GATE_NORM = 16.0
D_FF = 2816
CONV_W = 3
EPS = 1e-6

LANES = 128
SUBLANES = 8
VMEM_LIMIT = 56 * 1024 * 1024

_C_QA, _C_KA, _C_VA = 0, 1024, 2048
_C_QG, _C_KG, _C_VG, _C_RG = 3072, 3584, 4096, 5120
_C_GA, _C_GB, _C_LR = 6144, 7168, 8192
_W_COLS = _C_LR + LANES

_Q_SCALE = DA_HEAD_DIM ** -0.5 * math.log2(math.e)

_NT = (((1,), (1,)), ((), ()))
_TN = (((0,), (0,)), ((), ()))


def _rms(x, w):
    return x * lax.rsqrt(jnp.mean(x * x, axis=-1, keepdims=True) + EPS) * w


def _params(n_axes):
    return pltpu.CompilerParams(
        dimension_semantics=("arbitrary",) * n_axes, vmem_limit_bytes=VMEM_LIMIT)


def _resident(shape):
    nd = len(shape)
    return pl.BlockSpec(shape, lambda *_: (0,) * nd, pipeline_mode=pl.Buffered(1))


def _inproj_kernel(x_ref, nw_ref, w_ref, cos_ref, sna_ref, snb_ref, wgk_ref, bgk_ref, *refs):
    (q_ref, k_ref, v_ref, vt_ref, kb_ref, qg_ref, kg_ref, vg_ref, rg_ref, gk_ref,
     ga_ref, gb_ref) = refs[-12:]
    xb = _rms(x_ref[...], nw_ref[...]).astype(BF16)

    def proj(c0, n):
        return jnp.dot(xb, w_ref[:, c0:c0 + n], preferred_element_type=F32)

    cos, sna, snb = cos_ref[...], sna_ref[...], snb_ref[...]

    def rope(y):
        return (y * cos + pltpu.roll(y, LANES - ROPE_HALF, 1) * sna
                + pltpu.roll(y, ROPE_HALF, 1) * snb)

    for blk in range(2):
        yq = proj(_C_QA + blk * 512, 512)
        yk = proj(_C_KA + blk * 512, 512)
        for h in range(4):
            cols = slice(blk * 512 + h * DA_ROW, blk * 512 + (h + 1) * DA_ROW)
            q_ref[:, cols] = (rope(yq[:, h * DA_ROW:(h + 1) * DA_ROW]) * _Q_SCALE).astype(BF16)
            kr = rope(yk[:, h * DA_ROW:(h + 1) * DA_ROW])
            k_ref[:, blk * 4 + h, :] = kr
            kb_ref[:, cols] = kr.astype(BF16)
        yv = proj(_C_VA + blk * 512, 512)
        for h in range(4):
            v_ref[:, blk * 4 + h, :] = yv[:, h * DA_ROW:(h + 1) * DA_ROW]
        vt_ref[blk * 512:(blk + 1) * 512, :] = yv.T.astype(BF16)
    qg_ref[...] = (proj(_C_QG, 512) * (GLA_DK ** -0.5)).astype(BF16)
    kg_ref[...] = proj(_C_KG, 512).astype(BF16)
    for c0, dst in ((_C_VG, vg_ref), (_C_RG, rg_ref), (_C_GA, ga_ref), (_C_GB, gb_ref)):
        for blk in range(2):
            dst[:, blk * 512:(blk + 1) * 512] = proj(c0 + blk * 512, 512).astype(BF16)
    lr = proj(_C_LR, LANES)
    z = jnp.dot(lr.astype(BF16), wgk_ref[...], preferred_element_type=F32) + bgk_ref[...]
    gk_ref[...] = (jnp.minimum(z, 0.0) - jnp.log1p(jnp.exp(-jnp.abs(z)))) * (1.0 / GLA_GATE_NORM)


def _rope_tables(pos):
    inv = ROPE_THETA ** (-jnp.arange(ROPE_HALF, dtype=F32) * 2.0 / ROPE_DIM)
    ang = pos.astype(F32)[:, None] * inv[None, :]
    cos, sin = jnp.cos(ang), jnp.sin(ang)
    n = pos.shape[0]
    rest = DA_HEAD_DIM - ROPE_DIM
    z8 = jnp.zeros((n, ROPE_HALF), F32)
    cos64 = jnp.concatenate([cos, cos, jnp.ones((n, rest), F32)], axis=-1)
    sna64 = jnp.concatenate([-sin, z8, jnp.zeros((n, rest), F32)], axis=-1)
    snb64 = jnp.concatenate([z8, sin, jnp.zeros((n, rest), F32)], axis=-1)
    return tuple(jnp.tile(t, (1, 2)) for t in (cos64, sna64, snb64))


def _inproj(x2d, seq_len, pos0, nw, w_cat, wgk, bgk, li, depth, kv_stack):
    T = x2d.shape[0]
    tm = min(512, T)
    period = max(seq_len, tm)
    pos = pos0 + jnp.arange(period, dtype=jnp.int32) % seq_len
    cos, sna, snb = _rope_tables(pos)
    nper = period // tm
    row = lambda w: pl.BlockSpec((tm, w), lambda i: (i, 0))
    tab = pl.BlockSpec((tm, LANES), lambda i: (i % nper, 0))
    stack = pl.BlockSpec((None, tm, DA_HEADS, DA_ROW), lambda i: (li, i, 0, 0))
    stack_shape = jax.ShapeDtypeStruct((depth, T, DA_HEADS, DA_ROW), F32)
    vt = pl.BlockSpec((None, D_MODEL, tm), lambda i: (i, 0, 0))
    vt_shape = jax.ShapeDtypeStruct((T // tm, D_MODEL, tm), BF16)
    widths = (1024, 1024, 512, 512, 1024, 1024, 512, 1024, 1024)
    dtypes = (BF16, BF16, BF16, BF16, BF16, BF16, F32, BF16, BF16)
    rows = [row(w) for w in widths]
    shapes = [jax.ShapeDtypeStruct((T, w), d) for w, d in zip(widths, dtypes)]
    prev = () if kv_stack is None else tuple(kv_stack)
    n_in = 8
    return pl.pallas_call(
        _inproj_kernel,
        grid=(T // tm,),
        in_specs=[row(D_MODEL), _resident((1, D_MODEL)), _resident((D_MODEL, _W_COLS)),
                  tab, tab, tab, _resident((LANES, 512)), _resident((1, 512))]
                 + [pl.BlockSpec(memory_space=pl.ANY)] * len(prev),
        out_specs=rows[:1] + [stack, stack, vt] + rows[1:],
        out_shape=shapes[:1] + [stack_shape, stack_shape, vt_shape] + shapes[1:],
        input_output_aliases={n_in + j: 1 + j for j in range(len(prev))},
        compiler_params=_params(1),
        name="inproj",
    )(x2d, nw, w_cat, cos, sna, snb, wgk, bgk, *prev)


def _split_q(q):
    lane = lax.broadcasted_iota(jnp.int32, q.shape, 1)
    zero = jnp.zeros_like(q)
    return jnp.concatenate([jnp.where(lane < DA_HEAD_DIM, q, zero),
                            jnp.where(lane < DA_HEAD_DIM, zero, q)], axis=0)


def _lambda(lq1, lk1, lq2, lk2, lam_init):
    return (jnp.exp(jnp.sum(lq1[...] * lk1[...], axis=1, keepdims=True))
            - jnp.exp(jnp.sum(lq2[...] * lk2[...], axis=1, keepdims=True)) + lam_init)


def _attn_finish(o12, lam, nw, lam_init, n):
    od = o12[:n] - lam * o12[n:]
    return (_rms(od, nw) * (1.0 - lam_init)).astype(BF16)


_ATTN_HEADS_PER_STEP = 4


_ONES_ROWS = 16
_ATTN_TILE = 512


def _attn_prompt_kernel(q_ref, k_ref, vt_ref, lq1, lk1, lq2, lk2, nwc_ref, o_ref,
                        q2_ref, sa_ref, sb_ref, m_ref, l_ref, acc_ref, *, tq, nh, lam_init):
    i = pl.program_id(2)
    vt_per_tile = tq // vt_ref.shape[-1]
    heads = [slice(h * DA_ROW, (h + 1) * DA_ROW) for h in range(nh)]
    for h, cols in enumerate(heads):
        q2_ref[h] = _split_q(q_ref[:, cols])
    m_ref[...] = jnp.full(m_ref.shape, -jnp.inf, F32)
    l_ref[...] = jnp.zeros(l_ref.shape, F32)
    acc_ref[...] = jnp.zeros(acc_ref.shape, F32)
    ones = jnp.ones((_ONES_ROWS, tq), BF16)

    def scores(j, dst):
        r = pl.multiple_of(j * tq, tq)
        for h, cols in enumerate(heads):
            dst[h] = lax.dot_general(k_ref[pl.ds(r, tq), cols], q2_ref[h], _NT,
                                     preferred_element_type=F32)

    def consume(j, src, masked):
        for h, cols in enumerate(heads):
            s = src[h]
            if masked:
                key = lax.broadcasted_iota(jnp.int32, s.shape, 0)
                qry = lax.broadcasted_iota(jnp.int32, s.shape, 1)
                s = jnp.where((key // CHUNK) <= ((qry % tq) // CHUNK), s, -jnp.inf)
            m_prev = m_ref[h]
            m_new = jnp.maximum(m_prev, jnp.max(s, axis=0, keepdims=True))
            alpha = jnp.exp2(m_prev - m_new)
            p = jnp.exp2(s - m_new)
            vtj = jnp.concatenate([vt_ref[j * vt_per_tile + u, cols, :]
                                   for u in range(vt_per_tile)], axis=1)
            vte = jnp.concatenate([vtj, ones], axis=0)
            pv = jnp.dot(vte, p.astype(BF16), preferred_element_type=F32)
            acc_ref[h] = alpha * acc_ref[h] + pv[:DA_ROW]
            l_ref[h] = alpha * l_ref[h] + pv[DA_ROW:DA_ROW + 1]
            m_ref[h] = m_new

    scores(0, sa_ref)

    def pair(t, carry):
        j = 2 * t
        scores(j + 1, sb_ref)
        consume(j, sa_ref, False)
        scores(j + 2, sa_ref)
        consume(j + 1, sb_ref, False)
        return carry
    lax.fori_loop(0, lax.shift_right_logical(i, 1), pair, 0)

    @pl.when(lax.bitwise_and(i, 1) == 0)
    def _():
        consume(i, sa_ref, True)

    @pl.when(lax.bitwise_and(i, 1) == 1)
    def _():
        scores(i, sb_ref)
        consume(i - 1, sa_ref, False)
        consume(i, sb_ref, True)

    lam = _lambda(lq1, lk1, lq2, lk2, lam_init)
    for h in range(nh):
        o12 = acc_ref[h] / l_ref[h]
        od = o12[:, :tq] - lam * o12[:, tq:]
        y = od * lax.rsqrt(jnp.mean(od * od, axis=0, keepdims=True) + EPS) * nwc_ref[...]
        o_ref[:, h * DA_ROW:(h + 1) * DA_ROW] = (y * (1.0 - lam_init)).T.astype(BF16)


def _attn_prompt(q, k, vt, lam_p, nw, lam_init):
    B, S, _ = q.shape
    n_vt, tv = vt.shape[1], vt.shape[-1]
    tq = _ATTN_TILE if (S % _ATTN_TILE == 0 and _ATTN_TILE % tv == 0) else tv
    nh = _ATTN_HEADS_PER_STEP
    w = nh * DA_ROW
    lam_spec = _resident((1, DA_HEAD_DIM))
    stat = pltpu.VMEM((nh, 1, 2 * tq), F32)
    once = pl.Buffered(1)
    return pl.pallas_call(
        functools.partial(_attn_prompt_kernel, tq=tq, nh=nh, lam_init=lam_init),
        grid=(B, DA_HEADS // nh, S // tq),
        in_specs=[pl.BlockSpec((None, tq, w), lambda b, h, i: (b, i, h)),
                  pl.BlockSpec((None, S, w), lambda b, h, i: (b, 0, h), pipeline_mode=once),
                  pl.BlockSpec((None, n_vt, w, tv), lambda b, h, i: (b, 0, h, 0),
                               pipeline_mode=once),
                  lam_spec, lam_spec, lam_spec, lam_spec, _resident((DA_ROW, 1))],
        out_specs=pl.BlockSpec((None, tq, w), lambda b, h, i: (b, i, h)),
        out_shape=jax.ShapeDtypeStruct((B, S, D_MODEL), BF16),
        scratch_shapes=[pltpu.VMEM((nh, 2 * tq, DA_ROW), BF16),
                        pltpu.VMEM((nh, tq, 2 * tq), F32), pltpu.VMEM((nh, tq, 2 * tq), F32),
                        stat, stat, pltpu.VMEM((nh, DA_ROW, 2 * tq), F32)],
        compiler_params=_params(3),
        name="attn_prompt",
    )(q, k, vt, *lam_p, nw.reshape(DA_ROW, 1))


def _attn_sample_kernel(q_ref, kn_ref, vn_ref, ck_ref, cv_ref, lq1, lk1, lq2, lk2, nw_ref, o_ref,
                        ckb_ref, cvb_ref, *, lam_init):
    L, P = q_ref.shape[0], ck_ref.shape[0]
    flat = lambda r: r[...].reshape(r.shape[0] * DA_HEADS, DA_ROW).astype(BF16)
    ckb_ref[...] = flat(ck_ref)
    cvb_ref[...] = flat(cv_ref)
    kn, vn = flat(kn_ref), flat(vn_ref)
    head_p = lax.broadcasted_iota(jnp.int32, (2 * L, P * DA_HEADS), 1) % DA_HEADS
    head_n = lax.broadcasted_iota(jnp.int32, (2 * L, L * DA_HEADS), 1) % DA_HEADS
    lam = _lambda(lq1, lk1, lq2, lk2, lam_init)
    for h in range(DA_HEADS):
        cols = slice(h * DA_ROW, (h + 1) * DA_ROW)
        q2 = _split_q(q_ref[:, cols])
        s_p = lax.dot_general(q2, ckb_ref[...], _NT, preferred_element_type=F32)
        s_n = lax.dot_general(q2, kn, _NT, preferred_element_type=F32)
        s_p = jnp.where(head_p == h, s_p, -jnp.inf)
        s_n = jnp.where(head_n == h, s_n, -jnp.inf)
        m = jnp.maximum(jnp.max(s_p, axis=1, keepdims=True), jnp.max(s_n, axis=1, keepdims=True))
        p_p = jnp.exp2(s_p - m)
        p_n = jnp.exp2(s_n - m)
        l = jnp.sum(p_p, axis=1, keepdims=True) + jnp.sum(p_n, axis=1, keepdims=True)
        acc = (jnp.dot(p_p.astype(BF16), cvb_ref[...], preferred_element_type=F32)
               + jnp.dot(p_n.astype(BF16), vn, preferred_element_type=F32))
        o_ref[:, cols] = _attn_finish(acc / l, lam, nw_ref[...], lam_init, L)


def _attn_sample(q, k_stack, v_stack, cache_k, cache_v, li, lam_p, nw, lam_init):
    B, L, _ = q.shape
    P = cache_k.shape[2]
    row = pl.BlockSpec((None, L, D_MODEL), lambda b: (b, 0, 0))
    new = pl.BlockSpec((None, L, DA_HEADS, DA_ROW), lambda b: (li, b, 0, 0))
    past = pl.BlockSpec((None, None, P, DA_HEADS, DA_ROW), lambda b: (li, b, 0, 0, 0))
    lam_spec = _resident((1, DA_HEAD_DIM))
    return pl.pallas_call(
        functools.partial(_attn_sample_kernel, lam_init=lam_init),
        grid=(B,),
        in_specs=[row, new, new, past, past, lam_spec, lam_spec, lam_spec, lam_spec,
                  _resident((1, DA_ROW))],
        out_specs=row,
        out_shape=jax.ShapeDtypeStruct((B, L, D_MODEL), BF16),
        scratch_shapes=[pltpu.VMEM((P * DA_HEADS, DA_ROW), BF16)] * 2,
        compiler_params=_params(1),
        name="attn_sample",
    )(q, k_stack, v_stack, cache_k, cache_v, *lam_p, nw)


def _gla_kernel(q_ref, k_ref, v_ref, g_ref, r_ref, s0_ref, nw_ref, o_ref, sout_ref,
                qs_ref, ks_ref, vs_ref, os_ref, S_ref, *, C, nchunk):
    A = C // SUBLANES
    i = pl.program_id(2)

    @pl.when(i == 0)
    def _():
        S_ref[...] = s0_ref[...]

    qs_ref[...] = q_ref[...].astype(F32)
    ks_ref[...] = k_ref[...].astype(F32)
    for half in range(GLA_DV // LANES):
        vs_ref[half] = v_ref[:, half * LANES:(half + 1) * LANES].astype(F32)

    def time_of(r):
        return (r % SUBLANES) * A + r // SUBLANES

    t_r = time_of(lax.broadcasted_iota(jnp.int32, (C, C), 0))
    t_c = time_of(lax.broadcasted_iota(jnp.int32, (C, C), 1))
    t_k = time_of(lax.broadcasted_iota(jnp.int32, (C, GLA_DK), 0))
    halves = []
    hsz = C // 2
    while hsz >= A:
        halves.append(hsz)
        hsz //= 2
    sel = [t_c <= t_r] + [t_c <= (t_r // (2 * h)) * (2 * h) + h - 1 for h in halves]
    sel = jnp.concatenate([jnp.where(m, 1.0, 0.0) for m in sel], axis=0).astype(BF16)
    same_parent = [(t_r // (2 * h)) == (t_c // (2 * h)) for h in halves]
    second_half = [(t_k // h) % 2 == 1 for h in halves]
    r8 = lax.broadcasted_iota(jnp.int32, (SUBLANES, C), 0)
    c8 = lax.broadcasted_iota(jnp.int32, (SUBLANES, C), 1)
    ones_cv = jnp.ones((C, GLA_DV), BF16)

    def load(ref, base):
        return jnp.concatenate(
            [ref[pl.ds(base + a, SUBLANES, stride=A), :] for a in range(A)], axis=0)

    chunks = range(nchunk)
    q = [load(qs_ref, c * C) for c in chunks]
    k = [load(ks_ref, c * C) for c in chunks]
    g = [load(g_ref, c * C) for c in chunks]
    v = [jnp.concatenate([load(vs_ref.at[half], c * C) for half in range(GLA_DV // LANES)],
                         axis=1).astype(BF16) for c in chunks]
    g_hi = [x.astype(BF16) for x in g]
    g_lo = [(x - hi.astype(F32)).astype(BF16) for x, hi in zip(g, g_hi)]
    sums = [jnp.dot(sel, jnp.concatenate([hi, lo], axis=1), preferred_element_type=F32)
            for hi, lo in zip(g_hi, g_lo)]
    sums = [x[:, :GLA_DK] + x[:, GLA_DK:] for x in sums]
    b = [x[:C] for x in sums]
    decay = [lax.dot_general(hi, ones_cv, _TN, preferred_element_type=F32)
             + lax.dot_general(lo, ones_cv, _TN, preferred_element_type=F32)
             for hi, lo in zip(g_hi, g_lo)]

    amat = []
    for c in chunks:
        am = jnp.zeros((C, C), F32)
        for lvl, h in enumerate(halves):
            x = b[c] - sums[c][(lvl + 1) * C:(lvl + 2) * C]
            sec = second_half[lvl]
            qh = jnp.where(sec, q[c] * jnp.exp(jnp.where(sec, x, 0.0)), 0.0).astype(BF16)
            kh = jnp.where(sec, 0.0, k[c] * jnp.exp(jnp.where(sec, 0.0, -x))).astype(BF16)
            al = lax.dot_general(qh, kh, _NT, preferred_element_type=F32)
            am = am + jnp.where(same_parent[lvl], al, 0.0)
        amat.append(am)
    for c in chunks:
        diag = []
        for a in range(A):
            ra = slice(a * SUBLANES, (a + 1) * SUBLANES)
            acc = jnp.zeros((SUBLANES, C), F32)
            for a2 in range(a + 1):
                rb = slice(a2 * SUBLANES, (a2 + 1) * SUBLANES)
                e = q[c][ra] * k[c][rb] * jnp.exp(b[c][ra] - b[c][rb])
                acc = acc + jnp.where(c8 == a2 * SUBLANES + r8,
                                      jnp.sum(e, axis=1, keepdims=True), 0.0)
            diag.append(acc)
        amat[c] = amat[c] + jnp.concatenate(diag, axis=0)
    o_intra = [jnp.dot(amat[c].astype(BF16), v[c], preferred_element_type=F32) for c in chunks]
    upd = [lax.dot_general((k[c] * jnp.exp(b[c][C - 1:C] - b[c])).astype(BF16), v[c], _TN,
                           preferred_element_type=F32) for c in chunks]
    q_dec = [(q[c] * jnp.exp(b[c])).astype(BF16) for c in chunks]

    S = S_ref[...]
    for c in chunks:
        o = o_intra[c] + jnp.dot(q_dec[c], S.astype(BF16), preferred_element_type=F32)
        S = jnp.exp(decay[c]) * S + upd[c]
        for a in range(A):
            for half in range(GLA_DV // LANES):
                os_ref[half, pl.ds(c * C + a, SUBLANES, stride=A), :] = (
                    o[a * SUBLANES:(a + 1) * SUBLANES, half * LANES:(half + 1) * LANES])
    S_ref[...] = S

    og = jnp.concatenate([os_ref[half] for half in range(GLA_DV // LANES)], axis=1)
    y = _rms(og, nw_ref[...])
    r = r_ref[...].astype(F32)
    o_ref[...] = (y * (r * jax.nn.sigmoid(r))).astype(BF16)

    @pl.when(i == pl.num_programs(2) - 1)
    def _():
        sout_ref[...] = S_ref[...]


def _gla(qg, kg, vg, gk, rg, s0, nw):
    B, L, _ = qg.shape
    C = CHUNK if L % CHUNK == 0 else L
    tc = min(512, L)
    dk = lambda b, h, i: (b, i, h)
    st = pl.BlockSpec((None, None, GLA_DK, GLA_DV), lambda b, h, i: (b, h, 0, 0))
    return pl.pallas_call(
        functools.partial(_gla_kernel, C=C, nchunk=tc // C),
        grid=(B, GLA_HEADS, L // tc),
        in_specs=[pl.BlockSpec((None, tc, GLA_DK), dk), pl.BlockSpec((None, tc, GLA_DK), dk),
                  pl.BlockSpec((None, tc, GLA_DV), dk), pl.BlockSpec((None, tc, GLA_DK), dk),
                  pl.BlockSpec((None, tc, GLA_DV), dk), st, _resident((1, GLA_DV))],
        out_specs=[pl.BlockSpec((None, tc, GLA_DV), dk), st],
        out_shape=[jax.ShapeDtypeStruct((B, L, D_MODEL), BF16),
                   jax.ShapeDtypeStruct((B, GLA_HEADS, GLA_DK, GLA_DV), F32)],
        scratch_shapes=[pltpu.VMEM((tc, GLA_DK), F32), pltpu.VMEM((tc, GLA_DK), F32),
                        pltpu.VMEM((GLA_DV // LANES, tc, LANES), F32),
                        pltpu.VMEM((GLA_DV // LANES, tc, LANES), F32),
                        pltpu.VMEM((GLA_DK, GLA_DV), F32)],
        compiler_params=_params(3),
        name="gla",
    )(qg, kg, vg, gk, rg, s0, nw)


_FF_BLK = 256


def _mix_kernel(oa_ref, ob_ref, ga_ref, gb_ref, x_ref, wo_ref, nw0_ref,
                nw1_ref, wup_ref, cw_ref, cb_ref, wdn_ref, past_ref, nw2_ref,
                out_ref, cs_ref, act_ref, carry_ref, *, tiles_per_seq):
    i = pl.program_id(0)
    tm = x_ref.shape[0]

    @pl.when(i % tiles_per_seq == 0)
    def _():
        carry_ref[SUBLANES - 2:SUBLANES, :] = past_ref[...]

    f = lambda r: r[...].astype(F32)
    merged = jax.nn.sigmoid(f(ga_ref)) * f(oa_ref) + jax.nn.sigmoid(f(gb_ref)) * f(ob_ref)
    y = jnp.dot(merged.astype(BF16), wo_ref[...], preferred_element_type=F32)
    h = x_ref[...] + _rms(y, nw0_ref[...])
    hb = _rms(h, nw1_ref[...]).astype(BF16)
    rowid = lax.broadcasted_iota(jnp.int32, (tm, _FF_BLK), 0)
    for blk in range(D_FF // _FF_BLK):
        cols = slice(blk * _FF_BLK, (blk + 1) * _FF_BLK)
        u = jnp.dot(hb, wup_ref[:, cols], preferred_element_type=F32)
        g = jnp.dot(hb, wup_ref[:, D_FF + blk * _FF_BLK:D_FF + (blk + 1) * _FF_BLK],
                    preferred_element_type=F32)
        prev2 = carry_ref[SUBLANES - 2:SUBLANES - 1, cols]
        prev1 = carry_ref[SUBLANES - 1:SUBLANES, cols]
        g1 = jnp.where(rowid == 0, prev1, pltpu.roll(g, 1, 0))
        g2 = jnp.where(rowid == 0, prev2, jnp.where(rowid == 1, prev1, pltpu.roll(g, 2, 0)))
        gc = cb_ref[:, cols] + cw_ref[0:1, cols] * g2 + cw_ref[1:2, cols] * g1 + cw_ref[2:3, cols] * g
        act_ref[:, cols] = (jax.nn.gelu(gc, approximate=True) * u).astype(BF16)
        carry_ref[:, cols] = g[tm - SUBLANES:tm, :]
        cs_ref[:, cols] = g[tm - (CONV_W - 1):tm, :]
    ffn = jnp.dot(act_ref[...], wdn_ref[...], preferred_element_type=F32)
    out_ref[...] = h + _rms(ffn, nw2_ref[...])


def _mix(oa, ob, ga, gb, x2d, seq_len, wo, nw0, nw1, wup, cw, cb, wdn, conv_past, nw2):
    T = x2d.shape[0]
    nseq = T // seq_len
    tm = min(512, seq_len)
    tps = seq_len // tm
    row = pl.BlockSpec((tm, D_MODEL), lambda i: (i, 0))
    st = pl.BlockSpec((None, CONV_W - 1, D_FF), lambda i: (i // tps, 0, 0))
    vec = _resident((1, D_MODEL))
    return pl.pallas_call(
        functools.partial(_mix_kernel, tiles_per_seq=tps),
        grid=(T // tm,),
        in_specs=[row, row, row, row, row, _resident((D_MODEL, D_MODEL)), vec,
                  vec, _resident((D_MODEL, 2 * D_FF)),
                  _resident((CONV_W, D_FF)), _resident((1, D_FF)), _resident((D_FF, D_MODEL)),
                  st, vec],
        out_specs=[row, st],
        out_shape=[jax.ShapeDtypeStruct((T, D_MODEL), F32),
                   jax.ShapeDtypeStruct((nseq, CONV_W - 1, D_FF), F32)],
        scratch_shapes=[pltpu.VMEM((tm, D_FF), BF16), pltpu.VMEM((SUBLANES, D_FF), F32)],
        compiler_params=_params(1),
        name="mix",
    )(oa, ob, ga, gb, x2d, wo, nw0, nw1, wup, cw, cb, wdn, conv_past, nw2)


def _layer(x2d, nseq, seq_len, pos0, caches, li, depth, kv_stack, s0, conv_past, p):
    (q, k, v, vt, kb, qg, kg, vg, rg, gk, ga, gb) = _inproj(
        x2d, seq_len, pos0, p["pre_mix_w"], p["w_cat"], p["w_gk2"], p["b_gk2"], li, depth,
        kv_stack)
    lam_init = 0.8 - 0.6 * math.exp(-0.3 * li)
    lam_p = (p["lq1"], p["lk1"], p["lq2"], p["lk2"])
    sq = lambda t: t.reshape(nseq, seq_len, t.shape[-1])
    if caches is None:
        vt = vt.reshape(nseq, vt.shape[0] // nseq, D_MODEL, vt.shape[-1])
        oa = _attn_prompt(sq(q), sq(kb), vt, lam_p, p["da_norm_w"], lam_init)
    else:
        oa = _attn_sample(sq(q), k, v, caches[0], caches[1], li, lam_p, p["da_norm_w"], lam_init)
    ob, s_new = _gla(sq(qg), sq(kg), sq(vg), sq(gk), sq(rg), s0, p["gla_norm_w"])
    T = x2d.shape[0]
    out, conv_new = _mix(oa.reshape(T, D_MODEL), ob.reshape(T, D_MODEL), ga, gb, x2d, seq_len,
                         p["w_o"], p["post_mix_w"], p["pre_ffn_w"], p["w_up"], p["conv_w"],
                         p["conv_b"], p["w_down"], conv_past, p["post_ffn_w"])
    return out, (k, v), s_new, conv_new


def kernel(x_prompt, x_sample, cache_k, cache_v, state_gla, state_conv, w_in, w_gk2, b_gk2, lambda_q1, lambda_k1, lambda_q2, lambda_k2, da_norm_w, gla_norm_w, w_o, pre_mix_w, post_mix_w, pre_ffn_w, post_ffn_w, w_up, conv_w, conv_b, w_down):
    B, S, _ = x_prompt.shape
    Bs, L, _ = x_sample.shape
    depth, _, P = cache_k.shape[:3]
    hp = x_prompt.reshape(B * S, D_MODEL)
    hs = x_sample.reshape(Bs * L, D_MODEL)
    s0_p = jnp.zeros((B, GLA_HEADS, GLA_DK, GLA_DV), F32)
    conv0_p = jnp.zeros((B, CONV_W - 1, D_FF), F32)
    c_lr = _C_LR - 2 * D_MODEL + GLA_RANK
    outs_p, outs_s = [], []
    kv_p = kv_s = None
    for li in range(depth):
        w = w_in[li]
        w_cat = jnp.concatenate(
            [w[:, :c_lr - GLA_RANK], w[:, c_lr:], w[:, c_lr - GLA_RANK:c_lr],
             jnp.zeros((D_MODEL, LANES - GLA_RANK), w.dtype)], axis=1).astype(BF16)
        p = dict(
            w_cat=w_cat,
            w_gk2=jnp.concatenate([w_gk2[li], jnp.zeros((LANES - GLA_RANK, w_gk2.shape[-1]),
                                                        w_gk2.dtype)], axis=0).astype(BF16),
            b_gk2=b_gk2[li][None, :],
            lq1=lambda_q1[li][None, :], lk1=lambda_k1[li][None, :],
            lq2=lambda_q2[li][None, :], lk2=lambda_k2[li][None, :],
            da_norm_w=da_norm_w[li][None, :], gla_norm_w=gla_norm_w[li][None, :],
            w_o=w_o[li].astype(BF16),
            pre_mix_w=pre_mix_w[li][None, :], post_mix_w=post_mix_w[li][None, :],
            pre_ffn_w=pre_ffn_w[li][None, :], post_ffn_w=post_ffn_w[li][None, :],
            w_up=w_up[li].astype(BF16), conv_w=conv_w[li], conv_b=conv_b[li][None, :],
            w_down=w_down[li].astype(BF16))
        hp, kv_p, *rest_p = _layer(hp, B, S, 0, None, li, depth, kv_p, s0_p, conv0_p, p)
        hs, kv_s, *rest_s = _layer(hs, Bs, L, P, (cache_k, cache_v), li, depth, kv_s,
                                   state_gla[li], state_conv[li], p)
        outs_p.append(rest_p)
        outs_s.append(rest_s)
    stack = lambda outs, j: jnp.stack([o[j] for o in outs])
    kv5 = lambda t, n, l: t.reshape(depth, n, l, DA_HEADS, DA_ROW)
    return (hp.reshape(B, S, D_MODEL), hs.reshape(Bs, L, D_MODEL),
            kv5(kv_p[0], B, S), kv5(kv_p[1], B, S), stack(outs_p, 0), stack(outs_p, 1),
            kv5(kv_s[0], Bs, L), kv5(kv_s[1], Bs, L), stack(outs_s, 0), stack(outs_s, 1))
```

```python
import functools
import math

import jax
import jax.numpy as jnp
from jax import lax
from jax.experimental import pallas as pl
from jax.experimental.pallas import tpu as pltpu

F32 = jnp.float32
BF16 = jnp.bfloat16

D_MODEL = 1024
CHUNK = 64
DA_HEADS = 8
DA_HEAD_DIM = 64
DA_ROW = 2 * DA_HEAD_DIM
ROPE_DIM = DA_HEAD_DIM // 4
ROPE_HALF = ROPE_DIM // 2
ROPE_THETA = 500000.0
GLA_HEADS = 4
GLA_DK = 128
GLA_DV = 256
GLA_RANK = 16
GLA_GATE_NORM = 16.0
D_FF = 2816
CONV_W = 3
EPS = 1e-6

LANES = 128
SUBLANES = 8
VMEM_LIMIT = 56 * 1024 * 1024

_C_QA, _C_KA, _C_VA = 0, 1024, 2048
_C_QG, _C_KG, _C_VG, _C_RG = 3072, 3584, 4096, 5120
_C_GA, _C_GB, _C_LR = 6144, 7168, 8192
_W_COLS = _C_LR + LANES

_Q_SCALE = DA_HEAD_DIM ** -0.5 * math.log2(math.e)

_NT = (((1,), (1,)), ((), ()))
_TN = (((0,), (0,)), ((), ()))


def _rms(x, w):
    return x * lax.rsqrt(jnp.mean(x * x, axis=-1, keepdims=True) + EPS) * w


def _params(n_axes):
    return pltpu.CompilerParams(
        dimension_semantics=("arbitrary",) * n_axes, vmem_limit_bytes=VMEM_LIMIT)


def _resident(shape):
    nd = len(shape)
    return pl.BlockSpec(shape, lambda *_: (0,) * nd, pipeline_mode=pl.Buffered(1))


def _inproj_kernel(x_ref, nw_ref, w_ref, cos_ref, sna_ref, snb_ref, wgk_ref, bgk_ref, *refs):
    (q_ref, k_ref, v_ref, vt_ref, kb_ref, qg_ref, kg_ref, vg_ref, rg_ref, gk_ref,
     ga_ref, gb_ref) = refs[-12:]
    xb = _rms(x_ref[...], nw_ref[...]).astype(BF16)

    def proj(c0, n):
        return jnp.dot(xb, w_ref[:, c0:c0 + n], preferred_element_type=F32)

    cos, sna, snb = cos_ref[...], sna_ref[...], snb_ref[...]

    def rope(y):
        return (y * cos + pltpu.roll(y, LANES - ROPE_HALF, 1) * sna
                + pltpu.roll(y, ROPE_HALF, 1) * snb)

    for blk in range(2):
        yq = proj(_C_QA + blk * 512, 512)
        yk = proj(_C_KA + blk * 512, 512)
        for h in range(4):
            cols = slice(blk * 512 + h * DA_ROW, blk * 512 + (h + 1) * DA_ROW)
            q_ref[:, cols] = (rope(yq[:, h * DA_ROW:(h + 1) * DA_ROW]) * _Q_SCALE).astype(BF16)
            kr = rope(yk[:, h * DA_ROW:(h + 1) * DA_ROW])
            k_ref[:, blk * 4 + h, :] = kr
            kb_ref[:, cols] = kr.astype(BF16)
        yv = proj(_C_VA + blk * 512, 512)
        for h in range(4):
            v_ref[:, blk * 4 + h, :] = yv[:, h * DA_ROW:(h + 1) * DA_ROW]
        vt_ref[blk * 512:(blk + 1) * 512, :] = yv.T.astype(BF16)
    qg_ref[...] = (proj(_C_QG, 512) * (GLA_DK ** -0.5)).astype(BF16)
    kg_ref[...] = proj(_C_KG, 512).astype(BF16)
    for c0, dst in ((_C_VG, vg_ref), (_C_RG, rg_ref), (_C_GA, ga_ref), (_C_GB, gb_ref)):
        for blk in range(2):
            dst[:, blk * 512:(blk + 1) * 512] = proj(c0 + blk * 512, 512).astype(BF16)
    lr = proj(_C_LR, LANES)
    z = jnp.dot(lr.astype(BF16), wgk_ref[...], preferred_element_type=F32) + bgk_ref[...]
    gk_ref[...] = (jnp.minimum(z, 0.0) - jnp.log1p(jnp.exp(-jnp.abs(z)))) * (1.0 / GLA_GATE_NORM)


def _rope_tables(pos):
    inv = ROPE_THETA ** (-jnp.arange(ROPE_HALF, dtype=F32) * 2.0 / ROPE_DIM)
    ang = pos.astype(F32)[:, None] * inv[None, :]
    cos, sin = jnp.cos(ang), jnp.sin(ang)
    n = pos.shape[0]
    rest = DA_HEAD_DIM - ROPE_DIM
    z8 = jnp.zeros((n, ROPE_HALF), F32)
    cos64 = jnp.concatenate([cos, cos, jnp.ones((n, rest), F32)], axis=-1)
    sna64 = jnp.concatenate([-sin, z8, jnp.zeros((n, rest), F32)], axis=-1)
    snb64 = jnp.concatenate([z8, sin, jnp.zeros((n, rest), F32)], axis=-1)
    return tuple(jnp.tile(t, (1, 2)) for t in (cos64, sna64, snb64))


def _inproj(x2d, seq_len, pos0, nw, w_cat, wgk, bgk, li, depth, kv_stack):
    T = x2d.shape[0]
    tm = min(512, T)
    period = max(seq_len, tm)
    pos = pos0 + jnp.arange(period, dtype=jnp.int32) % seq_len
    cos, sna, snb = _rope_tables(pos)
    nper = period // tm
    row = lambda w: pl.BlockSpec((tm, w), lambda i: (i, 0))
    tab = pl.BlockSpec((tm, LANES), lambda i: (i % nper, 0))
    stack = pl.BlockSpec((None, tm, DA_HEADS, DA_ROW), lambda i: (li, i, 0, 0))
    stack_shape = jax.ShapeDtypeStruct((depth, T, DA_HEADS, DA_ROW), F32)
    vt = pl.BlockSpec((None, D_MODEL, tm), lambda i: (i, 0, 0))
    vt_shape = jax.ShapeDtypeStruct((T // tm, D_MODEL, tm), BF16)
    widths = (1024, 1024, 512, 512, 1024, 1024, 512, 1024, 1024)
    dtypes = (BF16, BF16, BF16, BF16, BF16, BF16, F32, BF16, BF16)
    rows = [row(w) for w in widths]
    shapes = [jax.ShapeDtypeStruct((T, w), d) for w, d in zip(widths, dtypes)]
    prev = () if kv_stack is None else tuple(kv_stack)
    n_in = 8
    return pl.pallas_call(
        _inproj_kernel,
        grid=(T // tm,),
        in_specs=[row(D_MODEL), _resident((1, D_MODEL)), _resident((D_MODEL, _W_COLS)),
                  tab, tab, tab, _resident((LANES, 512)), _resident((1, 512))]
                 + [pl.BlockSpec(memory_space=pl.ANY)] * len(prev),
        out_specs=rows[:1] + [stack, stack, vt] + rows[1:],
        out_shape=shapes[:1] + [stack_shape, stack_shape, vt_shape] + shapes[1:],
        input_output_aliases={n_in + j: 1 + j for j in range(len(prev))},
        compiler_params=_params(1),
        name="inproj",
    )(x2d, nw, w_cat, cos, sna, snb, wgk, bgk, *prev)


def _split_q(q):
    lane = lax.broadcasted_iota(jnp.int32, q.shape, 1)
    zero = jnp.zeros_like(q)
    return jnp.concatenate([jnp.where(lane < DA_HEAD_DIM, q, zero),
                            jnp.where(lane < DA_HEAD_DIM, zero, q)], axis=0)


def _lambda(lq1, lk1, lq2, lk2, lam_init):
    return (jnp.exp(jnp.sum(lq1[...] * lk1[...], axis=1, keepdims=True))
            - jnp.exp(jnp.sum(lq2[...] * lk2[...], axis=1, keepdims=True)) + lam_init)


def _attn_finish(o12, lam, nw, lam_init, n):
    od = o12[:n] - lam * o12[n:]
    return (_rms(od, nw) * (1.0 - lam_init)).astype(BF16)


_ATTN_HEADS_PER_STEP = 4


_ONES_ROWS = 16
_ATTN_TILE = 512


def _attn_prompt_kernel(q_ref, k_ref, vt_ref, lq1, lk1, lq2, lk2, nwc_ref, o_ref,
                        q2_ref, sa_ref, sb_ref, m_ref, l_ref, acc_ref, *, tq, nh, lam_init):
    i = pl.program_id(2)
    vt_per_tile = tq // vt_ref.shape[-1]
    heads = [slice(h * DA_ROW, (h + 1) * DA_ROW) for h in range(nh)]
    for h, cols in enumerate(heads):
        q2_ref[h] = _split_q(q_ref[:, cols])
    m_ref[...] = jnp.full(m_ref.shape, -jnp.inf, F32)
    l_ref[...] = jnp.zeros(l_ref.shape, F32)
    acc_ref[...] = jnp.zeros(acc_ref.shape, F32)
    ones = jnp.ones((_ONES_ROWS, tq), BF16)

    def scores(j, dst):
        r = pl.multiple_of(j * tq, tq)
        for h, cols in enumerate(heads):
            dst[h] = lax.dot_general(k_ref[pl.ds(r, tq), cols], q2_ref[h], _NT,
                                     preferred_element_type=F32)

    def consume(j, src, masked):
        for h, cols in enumerate(heads):
            s = src[h]
            if masked:
                key = lax.broadcasted_iota(jnp.int32, s.shape, 0)
                qry = lax.broadcasted_iota(jnp.int32, s.shape, 1)
                s = jnp.where((key // CHUNK) <= ((qry % tq) // CHUNK), s, -jnp.inf)
            m_prev = m_ref[h]
            m_new = jnp.maximum(m_prev, jnp.max(s, axis=0, keepdims=True))
            alpha = jnp.exp2(m_prev - m_new)
            p = jnp.exp2(s - m_new)
            vtj = jnp.concatenate([vt_ref[j * vt_per_tile + u, cols, :]
                                   for u in range(vt_per_tile)], axis=1)
            vte = jnp.concatenate([vtj, ones], axis=0)
            pv = jnp.dot(vte, p.astype(BF16), preferred_element_type=F32)
            acc_ref[h] = alpha * acc_ref[h] + pv[:DA_ROW]
            l_ref[h] = alpha * l_ref[h] + pv[DA_ROW:DA_ROW + 1]
            m_ref[h] = m_new

    scores(0, sa_ref)

    def pair(t, carry):
        j = 2 * t
        scores(j + 1, sb_ref)
        consume(j, sa_ref, False)
        scores(j + 2, sa_ref)
        consume(j + 1, sb_ref, False)
        return carry
    lax.fori_loop(0, lax.shift_right_logical(i, 1), pair, 0)

    @pl.when(lax.bitwise_and(i, 1) == 0)
    def _():
        consume(i, sa_ref, True)

    @pl.when(lax.bitwise_and(i, 1) == 1)
    def _():
        scores(i, sb_ref)
        consume(i - 1, sa_ref, False)
        consume(i, sb_ref, True)

    lam = _lambda(lq1, lk1, lq2, lk2, lam_init)
    for h in range(nh):
        o12 = acc_ref[h] / l_ref[h]
        od = o12[:, :tq] - lam * o12[:, tq:]
        y = od * lax.rsqrt(jnp.mean(od * od, axis=0, keepdims=True) + EPS) * nwc_ref[...]
        o_ref[:, h * DA_ROW:(h + 1) * DA_ROW] = (y * (1.0 - lam_init)).T.astype(BF16)


def _attn_prompt(q, k, vt, lam_p, nw, lam_init):
    B, S, _ = q.shape
    n_vt, tv = vt.shape[1], vt.shape[-1]
    tq = _ATTN_TILE if (S % _ATTN_TILE == 0 and _ATTN_TILE % tv == 0) else tv
    nh = _ATTN_HEADS_PER_STEP
    w = nh * DA_ROW
    lam_spec = _resident((1, DA_HEAD_DIM))
    stat = pltpu.VMEM((nh, 1, 2 * tq), F32)
    once = pl.Buffered(1)
    return pl.pallas_call(
        functools.partial(_attn_prompt_kernel, tq=tq, nh=nh, lam_init=lam_init),
        grid=(B, DA_HEADS // nh, S // tq),
        in_specs=[pl.BlockSpec((None, tq, w), lambda b, h, i: (b, i, h)),
                  pl.BlockSpec((None, S, w), lambda b, h, i: (b, 0, h), pipeline_mode=once),
                  pl.BlockSpec((None, n_vt, w, tv), lambda b, h, i: (b, 0, h, 0),
                               pipeline_mode=once),
                  lam_spec, lam_spec, lam_spec, lam_spec, _resident((DA_ROW, 1))],
        out_specs=pl.BlockSpec((None, tq, w), lambda b, h, i: (b, i, h)),
        out_shape=jax.ShapeDtypeStruct((B, S, D_MODEL), BF16),
        scratch_shapes=[pltpu.VMEM((nh, 2 * tq, DA_ROW), BF16),
                        pltpu.VMEM((nh, tq, 2 * tq), F32), pltpu.VMEM((nh, tq, 2 * tq), F32),
                        stat, stat, pltpu.VMEM((nh, DA_ROW, 2 * tq), F32)],
        compiler_params=_params(3),
        name="attn_prompt",
    )(q, k, vt, *lam_p, nw.reshape(DA_ROW, 1))


_SAMPLE_KV_ROWS = 2048


def _attn_sample_kernel(q_ref, kn_ref, vn_ref, ck_ref, cv_ref, lq1, lk1, lq2, lk2, nw_ref, o_ref,
                        ckb_ref, cvb_ref, *, lam_init):
    L, P = q_ref.shape[0], ck_ref.shape[0]
    n_past = P * DA_HEADS
    ch = min(_SAMPLE_KV_ROWS, n_past)
    rows = 2 * L * DA_HEADS
    flat = lambda r: r[...].reshape(r.shape[0] * DA_HEADS, DA_ROW).astype(BF16)
    ckb_ref[...] = flat(ck_ref)
    cvb_ref[...] = flat(cv_ref)
    kn, vn = flat(kn_ref), flat(vn_ref)
    q_all = jnp.concatenate([_split_q(q_ref[:, h * DA_ROW:(h + 1) * DA_ROW])
                             for h in range(DA_HEADS)], axis=0)

    def own_head(n_cols):
        row_head = lax.broadcasted_iota(jnp.int32, (rows, n_cols), 0) // (2 * L)
        col_head = lax.broadcasted_iota(jnp.int32, (rows, n_cols), 1) % DA_HEADS
        return row_head == col_head

    def update(carry, kc, vc, visible):
        m, l, acc = carry
        s = lax.dot_general(q_all, kc, _NT, preferred_element_type=F32)
        s = jnp.where(visible, s, -jnp.inf)
        m_new = jnp.maximum(m, jnp.max(s, axis=1, keepdims=True))
        alpha = jnp.exp2(m - m_new)
        p = jnp.exp2(s - m_new)
        return (m_new, alpha * l + jnp.sum(p, axis=1, keepdims=True),
                alpha * acc + jnp.dot(p.astype(BF16), vc, preferred_element_type=F32))

    vis_p = own_head(ch)

    def chunk(c, carry):
        r = pl.multiple_of(c * ch, ch)
        return update(carry, ckb_ref[pl.ds(r, ch), :], cvb_ref[pl.ds(r, ch), :], vis_p)

    init = (jnp.full((rows, 1), -jnp.inf, F32), jnp.zeros((rows, 1), F32),
            jnp.zeros((rows, DA_ROW), F32))
    carry = lax.fori_loop(0, n_past // ch, chunk, init)
    _, l, acc = update(carry, kn, vn, own_head(L * DA_HEADS))
    o12 = acc / l
    lam = _lambda(lq1, lk1, lq2, lk2, lam_init)
    for h in range(DA_HEADS):
        o_ref[:, h * DA_ROW:(h + 1) * DA_ROW] = _attn_finish(
            o12[h * 2 * L:(h + 1) * 2 * L], lam, nw_ref[...], lam_init, L)


def _attn_sample(q, k_stack, v_stack, cache_k, cache_v, li, lam_p, nw, lam_init):
    B, L, _ = q.shape
    P = cache_k.shape[2]
    row = pl.BlockSpec((None, L, D_MODEL), lambda b: (b, 0, 0))
    new = pl.BlockSpec((None, L, DA_HEADS, DA_ROW), lambda b: (li, b, 0, 0))
    past = pl.BlockSpec((None, None, P, DA_HEADS, DA_ROW), lambda b: (li, b, 0, 0, 0))
    lam_spec = _resident((1, DA_HEAD_DIM))
    return pl.pallas_call(
        functools.partial(_attn_sample_kernel, lam_init=lam_init),
        grid=(B,),
        in_specs=[row, new, new, past, past, lam_spec, lam_spec, lam_spec, lam_spec,
                  _resident((1, DA_ROW))],
        out_specs=row,
        out_shape=jax.ShapeDtypeStruct((B, L, D_MODEL), BF16),
        scratch_shapes=[pltpu.VMEM((P * DA_HEADS, DA_ROW), BF16)] * 2,
        compiler_params=_params(1),
        name="attn_sample",
    )(q, k_stack, v_stack, cache_k, cache_v, *lam_p, nw)


def _gla_kernel(q_ref, k_ref, v_ref, g_ref, r_ref, s0_ref, nw_ref, o_ref, sout_ref,
                qs_ref, ks_ref, vs_ref, os_ref, S_ref, *, C, nchunk):
    A = C // SUBLANES
    i = pl.program_id(2)

    @pl.when(i == 0)
    def _():
        S_ref[...] = s0_ref[...]

    qs_ref[...] = q_ref[...].astype(F32)
    ks_ref[...] = k_ref[...].astype(F32)
    for half in range(GLA_DV // LANES):
        vs_ref[half] = v_ref[:, half * LANES:(half + 1) * LANES].astype(F32)

    def time_of(r):
        return (r % SUBLANES) * A + r // SUBLANES

    t_r = time_of(lax.broadcasted_iota(jnp.int32, (C, C), 0))
    t_c = time_of(lax.broadcasted_iota(jnp.int32, (C, C), 1))
    t_k = time_of(lax.broadcasted_iota(jnp.int32, (C, GLA_DK), 0))
    halves = []
    hsz = C // 2
    while hsz >= A:
        halves.append(hsz)
        hsz //= 2
    sel = [t_c <= t_r] + [t_c <= (t_r // (2 * h)) * (2 * h) + h - 1 for h in halves]
    sel = jnp.concatenate([jnp.where(m, 1.0, 0.0) for m in sel], axis=0).astype(BF16)
    same_parent = [(t_r // (2 * h)) == (t_c // (2 * h)) for h in halves]
    second_half = [(t_k // h) % 2 == 1 for h in halves]
    r8 = lax.broadcasted_iota(jnp.int32, (SUBLANES, C), 0)
    c8 = lax.broadcasted_iota(jnp.int32, (SUBLANES, C), 1)
    ones_cv = jnp.ones((C, GLA_DV), BF16)

    def load(ref, base):
        return jnp.concatenate(
            [ref[pl.ds(base + a, SUBLANES, stride=A), :] for a in range(A)], axis=0)

    chunks = range(nchunk)
    q = [load(qs_ref, c * C) for c in chunks]
    k = [load(ks_ref, c * C) for c in chunks]
    g = [load(g_ref, c * C) for c in chunks]
    v = [jnp.concatenate([load(vs_ref.at[half], c * C) for half in range(GLA_DV // LANES)],
                         axis=1).astype(BF16) for c in chunks]
    g_hi = [x.astype(BF16) for x in g]
    g_lo = [(x - hi.astype(F32)).astype(BF16) for x, hi in zip(g, g_hi)]
    sums = [jnp.dot(sel, jnp.concatenate([hi, lo], axis=1), preferred_element_type=F32)
            for hi, lo in zip(g_hi, g_lo)]
    sums = [x[:, :GLA_DK] + x[:, GLA_DK:] for x in sums]
    b = [x[:C] for x in sums]
    decay = [lax.dot_general(hi, ones_cv, _TN, preferred_element_type=F32)
             + lax.dot_general(lo, ones_cv, _TN, preferred_element_type=F32)
             for hi, lo in zip(g_hi, g_lo)]

    amat = []
    for c in chunks:
        am = jnp.zeros((C, C), F32)
        for lvl, h in enumerate(halves):
            x = b[c] - sums[c][(lvl + 1) * C:(lvl + 2) * C]
            sec = second_half[lvl]
            qh = jnp.where(sec, q[c] * jnp.exp(jnp.where(sec, x, 0.0)), 0.0).astype(BF16)
            kh = jnp.where(sec, 0.0, k[c] * jnp.exp(jnp.where(sec, 0.0, -x))).astype(BF16)
            al = lax.dot_general(qh, kh, _NT, preferred_element_type=F32)
            am = am + jnp.where(same_parent[lvl], al, 0.0)
        amat.append(am)
    for c in chunks:
        diag = []
        for a in range(A):
            ra = slice(a * SUBLANES, (a + 1) * SUBLANES)
            acc = jnp.zeros((SUBLANES, C), F32)
            for a2 in range(a + 1):
                rb = slice(a2 * SUBLANES, (a2 + 1) * SUBLANES)
                e = q[c][ra] * k[c][rb] * jnp.exp(b[c][ra] - b[c][rb])
                acc = acc + jnp.where(c8 == a2 * SUBLANES + r8,
                                      jnp.sum(e, axis=1, keepdims=True), 0.0)
            diag.append(acc)
        amat[c] = amat[c] + jnp.concatenate(diag, axis=0)
    o_intra = [jnp.dot(amat[c].astype(BF16), v[c], preferred_element_type=F32) for c in chunks]
    upd = [lax.dot_general((k[c] * jnp.exp(b[c][C - 1:C] - b[c])).astype(BF16), v[c], _TN,
                           preferred_element_type=F32) for c in chunks]
    q_dec = [(q[c] * jnp.exp(b[c])).astype(BF16) for c in chunks]

    S = S_ref[...]
    for c in chunks:
        o = o_intra[c] + jnp.dot(q_dec[c], S.astype(BF16), preferred_element_type=F32)
        S = jnp.exp(decay[c]) * S + upd[c]
        for a in range(A):
            for half in range(GLA_DV // LANES):
                os_ref[half, pl.ds(c * C + a, SUBLANES, stride=A), :] = (
                    o[a * SUBLANES:(a + 1) * SUBLANES, half * LANES:(half + 1) * LANES])
    S_ref[...] = S

    og = jnp.concatenate([os_ref[half] for half in range(GLA_DV // LANES)], axis=1)
    y = _rms(og, nw_ref[...])
    r = r_ref[...].astype(F32)
    o_ref[...] = (y * (r * jax.nn.sigmoid(r))).astype(BF16)

    @pl.when(i == pl.num_programs(2) - 1)
    def _():
        sout_ref[...] = S_ref[...]


def _gla(qg, kg, vg, gk, rg, s0, nw):
    B, L, _ = qg.shape
    C = CHUNK if L % CHUNK == 0 else L
    tc = min(512, L)
    dk = lambda b, h, i: (b, i, h)
    st = pl.BlockSpec((None, None, GLA_DK, GLA_DV), lambda b, h, i: (b, h, 0, 0))
    return pl.pallas_call(
        functools.partial(_gla_kernel, C=C, nchunk=tc // C),
        grid=(B, GLA_HEADS, L // tc),
        in_specs=[pl.BlockSpec((None, tc, GLA_DK), dk), pl.BlockSpec((None, tc, GLA_DK), dk),
                  pl.BlockSpec((None, tc, GLA_DV), dk), pl.BlockSpec((None, tc, GLA_DK), dk),
                  pl.BlockSpec((None, tc, GLA_DV), dk), st, _resident((1, GLA_DV))],
        out_specs=[pl.BlockSpec((None, tc, GLA_DV), dk), st],
        out_shape=[jax.ShapeDtypeStruct((B, L, D_MODEL), BF16),
                   jax.ShapeDtypeStruct((B, GLA_HEADS, GLA_DK, GLA_DV), F32)],
        scratch_shapes=[pltpu.VMEM((tc, GLA_DK), F32), pltpu.VMEM((tc, GLA_DK), F32),
                        pltpu.VMEM((GLA_DV // LANES, tc, LANES), F32),
                        pltpu.VMEM((GLA_DV // LANES, tc, LANES), F32),
                        pltpu.VMEM((GLA_DK, GLA_DV), F32)],
        compiler_params=_params(3),
        name="gla",
    )(qg, kg, vg, gk, rg, s0, nw)


_FF_BLK = 256


def _mix_kernel(oa_ref, ob_ref, ga_ref, gb_ref, x_ref, wo_ref, nw0_ref,
                nw1_ref, wup_ref, cw_ref, cb_ref, wdn_ref, past_ref, nw2_ref,
                out_ref, cs_ref, act_ref, carry_ref, *, tiles_per_seq):
    i = pl.program_id(0)
    tm = x_ref.shape[0]

    @pl.when(i % tiles_per_seq == 0)
    def _():
        carry_ref[SUBLANES - 2:SUBLANES, :] = past_ref[...]

    f = lambda r: r[...].astype(F32)
    merged = jax.nn.sigmoid(f(ga_ref)) * f(oa_ref) + jax.nn.sigmoid(f(gb_ref)) * f(ob_ref)
    y = jnp.dot(merged.astype(BF16), wo_ref[...], preferred_element_type=F32)
    h = x_ref[...] + _rms(y, nw0_ref[...])
    hb = _rms(h, nw1_ref[...]).astype(BF16)
    rowid = lax.broadcasted_iota(jnp.int32, (tm, _FF_BLK), 0)
    for blk in range(D_FF // _FF_BLK):
        cols = slice(blk * _FF_BLK, (blk + 1) * _FF_BLK)
        u = jnp.dot(hb, wup_ref[:, cols], preferred_element_type=F32)
        g = jnp.dot(hb, wup_ref[:, D_FF + blk * _FF_BLK:D_FF + (blk + 1) * _FF_BLK],
                    preferred_element_type=F32)
        prev2 = carry_ref[SUBLANES - 2:SUBLANES - 1, cols]
        prev1 = carry_ref[SUBLANES - 1:SUBLANES, cols]
        g1 = jnp.where(rowid == 0, prev1, pltpu.roll(g, 1, 0))
        g2 = jnp.where(rowid == 0, prev2, jnp.where(rowid == 1, prev1, pltpu.roll(g, 2, 0)))
        gc = cb_ref[:, cols] + cw_ref[0:1, cols] * g2 + cw_ref[1:2, cols] * g1 + cw_ref[2:3, cols] * g
        act_ref[:, cols] = (jax.nn.gelu(gc, approximate=True) * u).astype(BF16)
        carry_ref[:, cols] = g[tm - SUBLANES:tm, :]
        cs_ref[:, cols] = g[tm - (CONV_W - 1):tm, :]
    ffn = jnp.dot(act_ref[...], wdn_ref[...], preferred_element_type=F32)
    out_ref[...] = h + _rms(ffn, nw2_ref[...])


def _mix(oa, ob, ga, gb, x2d, seq_len, wo, nw0, nw1, wup, cw, cb, wdn, conv_past, nw2):
    T = x2d.shape[0]
    nseq = T // seq_len
    tm = min(512, seq_len)
    tps = seq_len // tm
    row = pl.BlockSpec((tm, D_MODEL), lambda i: (i, 0))
    st = pl.BlockSpec((None, CONV_W - 1, D_FF), lambda i: (i // tps, 0, 0))
    vec = _resident((1, D_MODEL))
    return pl.pallas_call(
        functools.partial(_mix_kernel, tiles_per_seq=tps),
        grid=(T // tm,),
        in_specs=[row, row, row, row, row, _resident((D_MODEL, D_MODEL)), vec,
                  vec, _resident((D_MODEL, 2 * D_FF)),
                  _resident((CONV_W, D_FF)), _resident((1, D_FF)), _resident((D_FF, D_MODEL)),
                  st, vec],
        out_specs=[row, st],
        out_shape=[jax.ShapeDtypeStruct((T, D_MODEL), F32),
                   jax.ShapeDtypeStruct((nseq, CONV_W - 1, D_FF), F32)],
        scratch_shapes=[pltpu.VMEM((tm, D_FF), BF16), pltpu.VMEM((SUBLANES, D_FF), F32)],
        compiler_params=_params(1),
        name="mix",
    )(oa, ob, ga, gb, x2d, wo, nw0, nw1, wup, cw, cb, wdn, conv_past, nw2)


def _layer(x2d, nseq, seq_len, pos0, caches, li, depth, kv_stack, s0, conv_past, p):
    (q, k, v, vt, kb, qg, kg, vg, rg, gk, ga, gb) = _inproj(
        x2d, seq_len, pos0, p["pre_mix_w"], p["w_cat"], p["w_gk2"], p["b_gk2"], li, depth,
        kv_stack)
    lam_init = 0.8 - 0.6 * math.exp(-0.3 * li)
    lam_p = (p["lq1"], p["lk1"], p["lq2"], p["lk2"])
    sq = lambda t: t.reshape(nseq, seq_len, t.shape[-1])
    if caches is None:
        vt = vt.reshape(nseq, vt.shape[0] // nseq, D_MODEL, vt.shape[-1])
        oa = _attn_prompt(sq(q), sq(kb), vt, lam_p, p["da_norm_w"], lam_init)
    else:
        oa = _attn_sample(sq(q), k, v, caches[0], caches[1], li, lam_p, p["da_norm_w"], lam_init)
    ob, s_new = _gla(sq(qg), sq(kg), sq(vg), sq(gk), sq(rg), s0, p["gla_norm_w"])
    T = x2d.shape[0]
    out, conv_new = _mix(oa.reshape(T, D_MODEL), ob.reshape(T, D_MODEL), ga, gb, x2d, seq_len,
                         p["w_o"], p["post_mix_w"], p["pre_ffn_w"], p["w_up"], p["conv_w"],
                         p["conv_b"], p["w_down"], conv_past, p["post_ffn_w"])
    return out, (k, v), s_new, conv_new


def kernel(x_prompt, x_sample, cache_k, cache_v, state_gla, state_conv, w_in, w_gk2, b_gk2, lambda_q1, lambda_k1, lambda_q2, lambda_k2, da_norm_w, gla_norm_w, w_o, pre_mix_w, post_mix_w, pre_ffn_w, post_ffn_w, w_up, conv_w, conv_b, w_down):
    B, S, _ = x_prompt.shape
    Bs, L, _ = x_sample.shape
    depth, _, P = cache_k.shape[:3]
    hp = x_prompt.reshape(B * S, D_MODEL)
    hs = x_sample.reshape(Bs * L, D_MODEL)
    s0_p = jnp.zeros((B, GLA_HEADS, GLA_DK, GLA_DV), F32)
    conv0_p = jnp.zeros((B, CONV_W - 1, D_FF), F32)
    c_lr = _C_LR - 2 * D_MODEL + GLA_RANK
    outs_p, outs_s = [], []
    kv_p = kv_s = None
    for li in range(depth):
        w = w_in[li]
        w_cat = jnp.concatenate(
            [w[:, :c_lr - GLA_RANK], w[:, c_lr:], w[:, c_lr - GLA_RANK:c_lr],
             jnp.zeros((D_MODEL, LANES - GLA_RANK), w.dtype)], axis=1).astype(BF16)
        p = dict(
            w_cat=w_cat,
            w_gk2=jnp.concatenate([w_gk2[li], jnp.zeros((LANES - GLA_RANK, w_gk2.shape[-1]),
                                                        w_gk2.dtype)], axis=0).astype(BF16),
            b_gk2=b_gk2[li][None, :],
            lq1=lambda_q1[li][None, :], lk1=lambda_k1[li][None, :],
            lq2=lambda_q2[li][None, :], lk2=lambda_k2[li][None, :],
            da_norm_w=da_norm_w[li][None, :], gla_norm_w=gla_norm_w[li][None, :],
            w_o=w_o[li].astype(BF16),
            pre_mix_w=pre_mix_w[li][None, :], post_mix_w=post_mix_w[li][None, :],
            pre_ffn_w=pre_ffn_w[li][None, :], post_ffn_w=post_ffn_w[li][None, :],
            w_up=w_up[li].astype(BF16), conv_w=conv_w[li], conv_b=conv_b[li][None, :],
            w_down=w_down[li].astype(BF16))
        hp, kv_p, *rest_p = _layer(hp, B, S, 0, None, li, depth, kv_p, s0_p, conv0_p, p)
        hs, kv_s, *rest_s = _layer(hs, Bs, L, P, (cache_k, cache_v), li, depth, kv_s,
                                   state_gla[li], state_conv[li], p)
        outs_p.append(rest_p)
        outs_s.append(rest_s)
    stack = lambda outs, j: jnp.stack([o[j] for o in outs])
    kv5 = lambda t, n, l: t.reshape(depth, n, l, DA_HEADS, DA_ROW)
    return (hp.reshape(B, S, D_MODEL), hs.reshape(Bs, L, D_MODEL),
            kv5(kv_p[0], B, S), kv5(kv_p[1], B, S), stack(outs_p, 0), stack(outs_p, 1),
            kv5(kv_s[0], Bs, L), kv5(kv_s[1], Bs, L), stack(outs_s, 0), stack(outs_s, 1))
```

```python
import functools
import math

import jax
import jax.numpy as jnp
from jax import lax
from jax.experimental import pallas as pl
from jax.experimental.pallas import tpu as pltpu

F32 = jnp.float32
BF16 = jnp.bfloat16

D_MODEL = 1024
CHUNK = 64
DA_HEADS = 8
DA_HEAD_DIM = 64
DA_ROW = 2 * DA_HEAD_DIM
ROPE_DIM = DA_HEAD_DIM // 4
ROPE_HALF = ROPE_DIM // 2
ROPE_THETA = 500000.0
GLA_HEADS = 4
GLA_DK = 128
GLA_DV = 256
GLA_RANK = 16
GLA_GATE_NORM = 16.0
D_FF = 2816
CONV_W = 3
EPS = 1e-6

LANES = 128
SUBLANES = 8
VMEM_LIMIT = 56 * 1024 * 1024

_C_QA, _C_KA, _C_VA = 0, 1024, 2048
_C_QG, _C_KG, _C_VG, _C_RG = 3072, 3584, 4096, 5120
_C_GA, _C_GB, _C_LR = 6144, 7168, 8192
_W_COLS = _C_LR + LANES

_Q_SCALE = DA_HEAD_DIM ** -0.5 * math.log2(math.e)

_NT = (((1,), (1,)), ((), ()))
_TN = (((0,), (0,)), ((), ()))


def _rms(x, w):
    return x * lax.rsqrt(jnp.mean(x * x, axis=-1, keepdims=True) + EPS) * w


def _params(n_axes):
    return pltpu.CompilerParams(
        dimension_semantics=("arbitrary",) * n_axes, vmem_limit_bytes=VMEM_LIMIT)


def _resident(shape):
    nd = len(shape)
    return pl.BlockSpec(shape, lambda *_: (0,) * nd, pipeline_mode=pl.Buffered(1))


def _inproj_kernel(x_ref, nw_ref, w_ref, cos_ref, sna_ref, snb_ref, wgk_ref, bgk_ref, *refs):
    (q_ref, k_ref, v_ref, vt_ref, kb_ref, qg_ref, kg_ref, vg_ref, rg_ref, gk_ref,
     ga_ref, gb_ref) = refs[-12:]
    xb = _rms(x_ref[...], nw_ref[...]).astype(BF16)

    def proj(c0, n):
        return jnp.dot(xb, w_ref[:, c0:c0 + n], preferred_element_type=F32)

    cos, sna, snb = cos_ref[...], sna_ref[...], snb_ref[...]

    def rope(y):
        return (y * cos + pltpu.roll(y, LANES - ROPE_HALF, 1) * sna
                + pltpu.roll(y, ROPE_HALF, 1) * snb)

    for blk in range(2):
        yq = proj(_C_QA + blk * 512, 512)
        yk = proj(_C_KA + blk * 512, 512)
        for h in range(4):
            cols = slice(blk * 512 + h * DA_ROW, blk * 512 + (h + 1) * DA_ROW)
            q_ref[:, cols] = (rope(yq[:, h * DA_ROW:(h + 1) * DA_ROW]) * _Q_SCALE).astype(BF16)
            kr = rope(yk[:, h * DA_ROW:(h + 1) * DA_ROW])
            k_ref[:, blk * 4 + h, :] = kr
            kb_ref[:, cols] = kr.astype(BF16)
        yv = proj(_C_VA + blk * 512, 512)
        for h in range(4):
            v_ref[:, blk * 4 + h, :] = yv[:, h * DA_ROW:(h + 1) * DA_ROW]
        vt_ref[blk * 512:(blk + 1) * 512, :] = yv.T.astype(BF16)
    qg_ref[...] = (proj(_C_QG, 512) * (GLA_DK ** -0.5)).astype(BF16)
    kg_ref[...] = proj(_C_KG, 512).astype(BF16)
    for c0, dst in ((_C_VG, vg_ref), (_C_RG, rg_ref), (_C_GA, ga_ref), (_C_GB, gb_ref)):
        for blk in range(2):
            dst[:, blk * 512:(blk + 1) * 512] = proj(c0 + blk * 512, 512).astype(BF16)
    lr = proj(_C_LR, LANES)
    z = jnp.dot(lr.astype(BF16), wgk_ref[...], preferred_element_type=F32) + bgk_ref[...]
    gk_ref[...] = (jnp.minimum(z, 0.0) - jnp.log1p(jnp.exp(-jnp.abs(z)))) * (1.0 / GLA_GATE_NORM)


def _rope_tables(pos):
    inv = ROPE_THETA ** (-jnp.arange(ROPE_HALF, dtype=F32) * 2.0 / ROPE_DIM)
    ang = pos.astype(F32)[:, None] * inv[None, :]
    cos, sin = jnp.cos(ang), jnp.sin(ang)
    n = pos.shape[0]
    rest = DA_HEAD_DIM - ROPE_DIM
    z8 = jnp.zeros((n, ROPE_HALF), F32)
    cos64 = jnp.concatenate([cos, cos, jnp.ones((n, rest), F32)], axis=-1)
    sna64 = jnp.concatenate([-sin, z8, jnp.zeros((n, rest), F32)], axis=-1)
    snb64 = jnp.concatenate([z8, sin, jnp.zeros((n, rest), F32)], axis=-1)
    return tuple(jnp.tile(t, (1, 2)) for t in (cos64, sna64, snb64))


def _inproj(x2d, seq_len, pos0, nw, w_cat, wgk, bgk, li, depth, kv_stack):
    T = x2d.shape[0]
    tm = min(512, T)
    period = max(seq_len, tm)
    pos = pos0 + jnp.arange(period, dtype=jnp.int32) % seq_len
    cos, sna, snb = _rope_tables(pos)
    nper = period // tm
    row = lambda w: pl.BlockSpec((tm, w), lambda i: (i, 0))
    tab = pl.BlockSpec((tm, LANES), lambda i: (i % nper, 0))
    stack = pl.BlockSpec((None, tm, DA_HEADS, DA_ROW), lambda i: (li, i, 0, 0))
    stack_shape = jax.ShapeDtypeStruct((depth, T, DA_HEADS, DA_ROW), F32)
    vt = pl.BlockSpec((None, D_MODEL, tm), lambda i: (i, 0, 0))
    vt_shape = jax.ShapeDtypeStruct((T // tm, D_MODEL, tm), BF16)
    widths = (1024, 1024, 512, 512, 1024, 1024, 512, 1024, 1024)
    dtypes = (BF16, BF16, BF16, BF16, BF16, BF16, F32, BF16, BF16)
    rows = [row(w) for w in widths]
    shapes = [jax.ShapeDtypeStruct((T, w), d) for w, d in zip(widths, dtypes)]
    prev = () if kv_stack is None else tuple(kv_stack)
    n_in = 8
    return pl.pallas_call(
        _inproj_kernel,
        grid=(T // tm,),
        in_specs=[row(D_MODEL), _resident((1, D_MODEL)), _resident((D_MODEL, _W_COLS)),
                  tab, tab, tab, _resident((LANES, 512)), _resident((1, 512))]
                 + [pl.BlockSpec(memory_space=pl.ANY)] * len(prev),
        out_specs=rows[:1] + [stack, stack, vt] + rows[1:],
        out_shape=shapes[:1] + [stack_shape, stack_shape, vt_shape] + shapes[1:],
        input_output_aliases={n_in + j: 1 + j for j in range(len(prev))},
        compiler_params=_params(1),
        name="inproj",
    )(x2d, nw, w_cat, cos, sna, snb, wgk, bgk, *prev)


def _split_q(q):
    lane = lax.broadcasted_iota(jnp.int32, q.shape, 1)
    zero = jnp.zeros_like(q)
    return jnp.concatenate([jnp.where(lane < DA_HEAD_DIM, q, zero),
                            jnp.where(lane < DA_HEAD_DIM, zero, q)], axis=0)


def _lambda(lq1, lk1, lq2, lk2, lam_init):
    return (jnp.exp(jnp.sum(lq1[...] * lk1[...], axis=1, keepdims=True))
            - jnp.exp(jnp.sum(lq2[...] * lk2[...], axis=1, keepdims=True)) + lam_init)


def _attn_finish(o12, lam, nw, lam_init, n):
    od = o12[:n] - lam * o12[n:]
    return (_rms(od, nw) * (1.0 - lam_init)).astype(BF16)


_ATTN_HEADS_PER_STEP = 4


_ONES_ROWS = 16
_ATTN_TILE = 512


def _attn_prompt_kernel(q_ref, k_ref, vt_ref, lq1, lk1, lq2, lk2, nwc_ref, o_ref,
                        q2_ref, sa_ref, sb_ref, m_ref, l_ref, acc_ref, *, tq, nh, lam_init):
    i = pl.program_id(2)
    vt_per_tile = tq // vt_ref.shape[-1]
    heads = [slice(h * DA_ROW, (h + 1) * DA_ROW) for h in range(nh)]
    comp = lax.broadcasted_iota(jnp.int32, (DA_ROW, tq), 0) // DA_HEAD_DIM
    for h, cols in enumerate(heads):
        qt = q_ref[:, cols].astype(F32).T
        q2_ref[h] = jnp.concatenate([jnp.where(comp == 0, qt, 0.0),
                                     jnp.where(comp == 0, 0.0, qt)], axis=1).astype(BF16)
    m_ref[...] = jnp.full(m_ref.shape, -jnp.inf, F32)
    l_ref[...] = jnp.zeros(l_ref.shape, F32)
    acc_ref[...] = jnp.zeros(acc_ref.shape, F32)
    ones = jnp.ones((_ONES_ROWS, tq), BF16)

    def scores(j, dst):
        r = pl.multiple_of(j * tq, tq)
        for h, cols in enumerate(heads):
            dst[h] = jnp.dot(k_ref[pl.ds(r, tq), cols], q2_ref[h],
                             preferred_element_type=F32)

    def consume(j, src, masked):
        for h, cols in enumerate(heads):
            s = src[h]
            if masked:
                key = lax.broadcasted_iota(jnp.int32, s.shape, 0)
                qry = lax.broadcasted_iota(jnp.int32, s.shape, 1)
                s = jnp.where((key // CHUNK) <= ((qry % tq) // CHUNK), s, -jnp.inf)
            m_prev = m_ref[h]
            m_new = jnp.maximum(m_prev, jnp.max(s, axis=0, keepdims=True))
            alpha = jnp.exp2(m_prev - m_new)
            p = jnp.exp2(s - m_new)
            vtj = jnp.concatenate([vt_ref[j * vt_per_tile + u, cols, :]
                                   for u in range(vt_per_tile)], axis=1)
            vte = jnp.concatenate([vtj, ones], axis=0)
            pv = jnp.dot(vte, p.astype(BF16), preferred_element_type=F32)
            acc_ref[h] = alpha * acc_ref[h] + pv[:DA_ROW]
            l_ref[h] = alpha * l_ref[h] + pv[DA_ROW:DA_ROW + 1]
            m_ref[h] = m_new

    scores(0, sa_ref)

    def pair(t, carry):
        j = 2 * t
        scores(j + 1, sb_ref)
        consume(j, sa_ref, False)
        scores(j + 2, sa_ref)
        consume(j + 1, sb_ref, False)
        return carry
    lax.fori_loop(0, lax.shift_right_logical(i, 1), pair, 0)

    @pl.when(lax.bitwise_and(i, 1) == 0)
    def _():
        consume(i, sa_ref, True)

    @pl.when(lax.bitwise_and(i, 1) == 1)
    def _():
        scores(i, sb_ref)
        consume(i - 1, sa_ref, False)
        consume(i, sb_ref, True)

    lam = _lambda(lq1, lk1, lq2, lk2, lam_init)
    for h in range(nh):
        o12 = acc_ref[h] / l_ref[h]
        od = o12[:, :tq] - lam * o12[:, tq:]
        y = od * lax.rsqrt(jnp.mean(od * od, axis=0, keepdims=True) + EPS) * nwc_ref[...]
        o_ref[:, h * DA_ROW:(h + 1) * DA_ROW] = (y * (1.0 - lam_init)).T.astype(BF16)


def _attn_prompt(q, k, vt, lam_p, nw, lam_init):
    B, S, _ = q.shape
    n_vt, tv = vt.shape[1], vt.shape[-1]
    tq = _ATTN_TILE if (S % _ATTN_TILE == 0 and _ATTN_TILE % tv == 0) else tv
    nh = _ATTN_HEADS_PER_STEP
    w = nh * DA_ROW
    lam_spec = _resident((1, DA_HEAD_DIM))
    stat = pltpu.VMEM((nh, 1, 2 * tq), F32)
    once = pl.Buffered(1)
    return pl.pallas_call(
        functools.partial(_attn_prompt_kernel, tq=tq, nh=nh, lam_init=lam_init),
        grid=(B, DA_HEADS // nh, S // tq),
        in_specs=[pl.BlockSpec((None, tq, w), lambda b, h, i: (b, i, h)),
                  pl.BlockSpec((None, S, w), lambda b, h, i: (b, 0, h)),
                  pl.BlockSpec((None, n_vt, w, tv), lambda b, h, i: (b, 0, h, 0),
                               pipeline_mode=once),
                  lam_spec, lam_spec, lam_spec, lam_spec, _resident((DA_ROW, 1))],
        out_specs=pl.BlockSpec((None, tq, w), lambda b, h, i: (b, i, h)),
        out_shape=jax.ShapeDtypeStruct((B, S, D_MODEL), BF16),
        scratch_shapes=[pltpu.VMEM((nh, DA_ROW, 2 * tq), BF16),
                        pltpu.VMEM((nh, tq, 2 * tq), F32), pltpu.VMEM((nh, tq, 2 * tq), F32),
                        stat, stat, pltpu.VMEM((nh, DA_ROW, 2 * tq), F32)],
        compiler_params=_params(3),
        name="attn_prompt",
    )(q, k, vt, *lam_p, nw.reshape(DA_ROW, 1))


_SAMPLE_KV_ROWS = 2048


def _attn_sample_kernel(q_ref, kn_ref, vn_ref, ck_ref, cv_ref, lq1, lk1, lq2, lk2, nw_ref, o_ref,
                        ckb_ref, cvb_ref, *, lam_init):
    L, P = q_ref.shape[0], ck_ref.shape[0]
    n_past = P * DA_HEADS
    ch = min(_SAMPLE_KV_ROWS, n_past)
    rows = 2 * L * DA_HEADS
    flat = lambda r: r[...].reshape(r.shape[0] * DA_HEADS, DA_ROW).astype(BF16)
    ckb_ref[...] = flat(ck_ref)
    cvb_ref[...] = flat(cv_ref)
    kn, vn = flat(kn_ref), flat(vn_ref)
    q_all = jnp.concatenate([_split_q(q_ref[:, h * DA_ROW:(h + 1) * DA_ROW])
                             for h in range(DA_HEADS)], axis=0)

    def own_head(n_cols):
        row_head = lax.broadcasted_iota(jnp.int32, (rows, n_cols), 0) // (2 * L)
        col_head = lax.broadcasted_iota(jnp.int32, (rows, n_cols), 1) % DA_HEADS
        return row_head == col_head

    def update(carry, kc, vc, visible):
        m, l, acc = carry
        s = lax.dot_general(q_all, kc, _NT, preferred_element_type=F32)
        s = jnp.where(visible, s, -jnp.inf)
        m_new = jnp.maximum(m, jnp.max(s, axis=1, keepdims=True))
        alpha = jnp.exp2(m - m_new)
        p = jnp.exp2(s - m_new)
        return (m_new, alpha * l + jnp.sum(p, axis=1, keepdims=True),
                alpha * acc + jnp.dot(p.astype(BF16), vc, preferred_element_type=F32))

    vis_p = own_head(ch)

    def chunk(c, carry):
        r = pl.multiple_of(c * ch, ch)
        return update(carry, ckb_ref[pl.ds(r, ch), :], cvb_ref[pl.ds(r, ch), :], vis_p)

    init = (jnp.full((rows, 1), -jnp.inf, F32), jnp.zeros((rows, 1), F32),
            jnp.zeros((rows, DA_ROW), F32))
    carry = lax.fori_loop(0, n_past // ch, chunk, init)
    _, l, acc = update(carry, kn, vn, own_head(L * DA_HEADS))
    o12 = acc / l
    lam = _lambda(lq1, lk1, lq2, lk2, lam_init)
    for h in range(DA_HEADS):
        o_ref[:, h * DA_ROW:(h + 1) * DA_ROW] = _attn_finish(
            o12[h * 2 * L:(h + 1) * 2 * L], lam, nw_ref[...], lam_init, L)


def _attn_sample(q, k_stack, v_stack, cache_k, cache_v, li, lam_p, nw, lam_init):
    B, L, _ = q.shape
    P = cache_k.shape[2]
    row = pl.BlockSpec((None, L, D_MODEL), lambda b: (b, 0, 0))
    new = pl.BlockSpec((None, L, DA_HEADS, DA_ROW), lambda b: (li, b, 0, 0))
    past = pl.BlockSpec((None, None, P, DA_HEADS, DA_ROW), lambda b: (li, b, 0, 0, 0))
    lam_spec = _resident((1, DA_HEAD_DIM))
    return pl.pallas_call(
        functools.partial(_attn_sample_kernel, lam_init=lam_init),
        grid=(B,),
        in_specs=[row, new, new, past, past, lam_spec, lam_spec, lam_spec, lam_spec,
                  _resident((1, DA_ROW))],
        out_specs=row,
        out_shape=jax.ShapeDtypeStruct((B, L, D_MODEL), BF16),
        scratch_shapes=[pltpu.VMEM((P * DA_HEADS, DA_ROW), BF16)] * 2,
        compiler_params=_params(1),
        name="attn_sample",
    )(q, k_stack, v_stack, cache_k, cache_v, *lam_p, nw)


def _gla_kernel(q_ref, k_ref, v_ref, g_ref, r_ref, s0_ref, nw_ref, o_ref, sout_ref,
                qs_ref, ks_ref, vs_ref, os_ref, S_ref, *, C, nchunk):
    A = C // SUBLANES
    i = pl.program_id(2)

    @pl.when(i == 0)
    def _():
        S_ref[...] = s0_ref[...]

    qs_ref[...] = q_ref[...].astype(F32)
    ks_ref[...] = k_ref[...].astype(F32)
    for half in range(GLA_DV // LANES):
        vs_ref[half] = v_ref[:, half * LANES:(half + 1) * LANES].astype(F32)

    def time_of(r):
        return (r % SUBLANES) * A + r // SUBLANES

    t_r = time_of(lax.broadcasted_iota(jnp.int32, (C, C), 0))
    t_c = time_of(lax.broadcasted_iota(jnp.int32, (C, C), 1))
    t_k = time_of(lax.broadcasted_iota(jnp.int32, (C, GLA_DK), 0))
    halves = []
    hsz = C // 2
    while hsz >= A:
        halves.append(hsz)
        hsz //= 2
    sel = [t_c <= t_r] + [t_c <= (t_r // (2 * h)) * (2 * h) + h - 1 for h in halves]
    sel = jnp.concatenate([jnp.where(m, 1.0, 0.0) for m in sel], axis=0).astype(BF16)
    same_parent = [(t_r // (2 * h)) == (t_c // (2 * h)) for h in halves]
    second_half = [(t_k // h) % 2 == 1 for h in halves]
    r8 = lax.broadcasted_iota(jnp.int32, (SUBLANES, C), 0)
    c8 = lax.broadcasted_iota(jnp.int32, (SUBLANES, C), 1)
    ones_cv = jnp.ones((C, GLA_DV), BF16)

    def load(ref, base):
        return jnp.concatenate(
            [ref[pl.ds(base + a, SUBLANES, stride=A), :] for a in range(A)], axis=0)

    chunks = range(nchunk)
    q = [load(qs_ref, c * C) for c in chunks]
    k = [load(ks_ref, c * C) for c in chunks]
    g = [load(g_ref, c * C) for c in chunks]
    v = [jnp.concatenate([load(vs_ref.at[half], c * C) for half in range(GLA_DV // LANES)],
                         axis=1).astype(BF16) for c in chunks]
    g_hi = [x.astype(BF16) for x in g]
    g_lo = [(x - hi.astype(F32)).astype(BF16) for x, hi in zip(g, g_hi)]
    sums = [jnp.dot(sel, jnp.concatenate([hi, lo], axis=1), preferred_element_type=F32)
            for hi, lo in zip(g_hi, g_lo)]
    sums = [x[:, :GLA_DK] + x[:, GLA_DK:] for x in sums]
    b = [x[:C] for x in sums]
    decay = [lax.dot_general(hi, ones_cv, _TN, preferred_element_type=F32)
             + lax.dot_general(lo, ones_cv, _TN, preferred_element_type=F32)
             for hi, lo in zip(g_hi, g_lo)]

    amat = []
    for c in chunks:
        am = jnp.zeros((C, C), F32)
        for lvl, h in enumerate(halves):
            x = b[c] - sums[c][(lvl + 1) * C:(lvl + 2) * C]
            sec = second_half[lvl]
            qh = jnp.where(sec, q[c] * jnp.exp(jnp.where(sec, x, 0.0)), 0.0).astype(BF16)
            kh = jnp.where(sec, 0.0, k[c] * jnp.exp(jnp.where(sec, 0.0, -x))).astype(BF16)
            al = lax.dot_general(qh, kh, _NT, preferred_element_type=F32)
            am = am + jnp.where(same_parent[lvl], al, 0.0)
        amat.append(am)
    for c in chunks:
        diag = []
        for a in range(A):
            ra = slice(a * SUBLANES, (a + 1) * SUBLANES)
            acc = jnp.zeros((SUBLANES, C), F32)
            for a2 in range(a + 1):
                rb = slice(a2 * SUBLANES, (a2 + 1) * SUBLANES)
                e = q[c][ra] * k[c][rb] * jnp.exp(b[c][ra] - b[c][rb])
                acc = acc + jnp.where(c8 == a2 * SUBLANES + r8,
                                      jnp.sum(e, axis=1, keepdims=True), 0.0)
            diag.append(acc)
        amat[c] = amat[c] + jnp.concatenate(diag, axis=0)
    o_intra = [jnp.dot(amat[c].astype(BF16), v[c], preferred_element_type=F32) for c in chunks]
    upd = [lax.dot_general((k[c] * jnp.exp(b[c][C - 1:C] - b[c])).astype(BF16), v[c], _TN,
                           preferred_element_type=F32) for c in chunks]
    q_dec = [(q[c] * jnp.exp(b[c])).astype(BF16) for c in chunks]

    S = S_ref[...]
    for c in chunks:
        o = o_intra[c] + jnp.dot(q_dec[c], S.astype(BF16), preferred_element_type=F32)
        S = jnp.exp(decay[c]) * S + upd[c]
        for a in range(A):
            for half in range(GLA_DV // LANES):
                os_ref[half, pl.ds(c * C + a, SUBLANES, stride=A), :] = (
                    o[a * SUBLANES:(a + 1) * SUBLANES, half * LANES:(half + 1) * LANES])
    S_ref[...] = S

    og = jnp.concatenate([os_ref[half] for half in range(GLA_DV // LANES)], axis=1)
    y = _rms(og, nw_ref[...])
    r = r_ref[...].astype(F32)
    o_ref[...] = (y * (r * jax.nn.sigmoid(r))).astype(BF16)

    @pl.when(i == pl.num_programs(2) - 1)
    def _():
        sout_ref[...] = S_ref[...]


def _gla(qg, kg, vg, gk, rg, s0, nw):
    B, L, _ = qg.shape
    C = CHUNK if L % CHUNK == 0 else L
    tc = min(512, L)
    dk = lambda b, h, i: (b, i, h)
    st = pl.BlockSpec((None, None, GLA_DK, GLA_DV), lambda b, h, i: (b, h, 0, 0))
    return pl.pallas_call(
        functools.partial(_gla_kernel, C=C, nchunk=tc // C),
        grid=(B, GLA_HEADS, L // tc),
        in_specs=[pl.BlockSpec((None, tc, GLA_DK), dk), pl.BlockSpec((None, tc, GLA_DK), dk),
                  pl.BlockSpec((None, tc, GLA_DV), dk), pl.BlockSpec((None, tc, GLA_DK), dk),
                  pl.BlockSpec((None, tc, GLA_DV), dk), st, _resident((1, GLA_DV))],
        out_specs=[pl.BlockSpec((None, tc, GLA_DV), dk), st],
        out_shape=[jax.ShapeDtypeStruct((B, L, D_MODEL), BF16),
                   jax.ShapeDtypeStruct((B, GLA_HEADS, GLA_DK, GLA_DV), F32)],
        scratch_shapes=[pltpu.VMEM((tc, GLA_DK), F32), pltpu.VMEM((tc, GLA_DK), F32),
                        pltpu.VMEM((GLA_DV // LANES, tc, LANES), F32),
                        pltpu.VMEM((GLA_DV // LANES, tc, LANES), F32),
                        pltpu.VMEM((GLA_DK, GLA_DV), F32)],
        compiler_params=_params(3),
        name="gla",
    )(qg, kg, vg, gk, rg, s0, nw)


_FF_BLK = 256


def _mix_kernel(oa_ref, ob_ref, ga_ref, gb_ref, x_ref, wo_ref, nw0_ref,
                nw1_ref, wup_ref, cw_ref, cb_ref, wdn_ref, past_ref, nw2_ref,
                out_ref, cs_ref, act_ref, carry_ref, *, tiles_per_seq):
    i = pl.program_id(0)
    tm = x_ref.shape[0]

    @pl.when(i % tiles_per_seq == 0)
    def _():
        carry_ref[SUBLANES - 2:SUBLANES, :] = past_ref[...]

    f = lambda r: r[...].astype(F32)
    merged = jax.nn.sigmoid(f(ga_ref)) * f(oa_ref) + jax.nn.sigmoid(f(gb_ref)) * f(ob_ref)
    y = jnp.dot(merged.astype(BF16), wo_ref[...], preferred_element_type=F32)
    h = x_ref[...] + _rms(y, nw0_ref[...])
    hb = _rms(h, nw1_ref[...]).astype(BF16)
    rowid = lax.broadcasted_iota(jnp.int32, (tm, _FF_BLK), 0)
    for blk in range(D_FF // _FF_BLK):
        cols = slice(blk * _FF_BLK, (blk + 1) * _FF_BLK)
        u = jnp.dot(hb, wup_ref[:, cols], preferred_element_type=F32)
        g = jnp.dot(hb, wup_ref[:, D_FF + blk * _FF_BLK:D_FF + (blk + 1) * _FF_BLK],
                    preferred_element_type=F32)
        prev2 = carry_ref[SUBLANES - 2:SUBLANES - 1, cols]
        prev1 = carry_ref[SUBLANES - 1:SUBLANES, cols]
        g1 = jnp.where(rowid == 0, prev1, pltpu.roll(g, 1, 0))
        g2 = jnp.where(rowid == 0, prev2, jnp.where(rowid == 1, prev1, pltpu.roll(g, 2, 0)))
        gc = cb_ref[:, cols] + cw_ref[0:1, cols] * g2 + cw_ref[1:2, cols] * g1 + cw_ref[2:3, cols] * g
        act_ref[:, cols] = (jax.nn.gelu(gc, approximate=True) * u).astype(BF16)
        carry_ref[:, cols] = g[tm - SUBLANES:tm, :]
        cs_ref[:, cols] = g[tm - (CONV_W - 1):tm, :]
    ffn = jnp.dot(act_ref[...], wdn_ref[...], preferred_element_type=F32)
    out_ref[...] = h + _rms(ffn, nw2_ref[...])


def _mix(oa, ob, ga, gb, x2d, seq_len, wo, nw0, nw1, wup, cw, cb, wdn, conv_past, nw2):
    T = x2d.shape[0]
    nseq = T // seq_len
    tm = min(512, seq_len)
    tps = seq_len // tm
    row = pl.BlockSpec((tm, D_MODEL), lambda i: (i, 0))
    st = pl.BlockSpec((None, CONV_W - 1, D_FF), lambda i: (i // tps, 0, 0))
    vec = _resident((1, D_MODEL))
    return pl.pallas_call(
        functools.partial(_mix_kernel, tiles_per_seq=tps),
        grid=(T // tm,),
        in_specs=[row, row, row, row, row, _resident((D_MODEL, D_MODEL)), vec,
                  vec, _resident((D_MODEL, 2 * D_FF)),
                  _resident((CONV_W, D_FF)), _resident((1, D_FF)), _resident((D_FF, D_MODEL)),
                  st, vec],
        out_specs=[row, st],
        out_shape=[jax.ShapeDtypeStruct((T, D_MODEL), F32),
                   jax.ShapeDtypeStruct((nseq, CONV_W - 1, D_FF), F32)],
        scratch_shapes=[pltpu.VMEM((tm, D_FF), BF16), pltpu.VMEM((SUBLANES, D_FF), F32)],
        compiler_params=_params(1),
        name="mix",
    )(oa, ob, ga, gb, x2d, wo, nw0, nw1, wup, cw, cb, wdn, conv_past, nw2)


def _layer(x2d, nseq, seq_len, pos0, caches, li, depth, kv_stack, s0, conv_past, p):
    (q, k, v, vt, kb, qg, kg, vg, rg, gk, ga, gb) = _inproj(
        x2d, seq_len, pos0, p["pre_mix_w"], p["w_cat"], p["w_gk2"], p["b_gk2"], li, depth,
        kv_stack)
    lam_init = 0.8 - 0.6 * math.exp(-0.3 * li)
    lam_p = (p["lq1"], p["lk1"], p["lq2"], p["lk2"])
    sq = lambda t: t.reshape(nseq, seq_len, t.shape[-1])
    if caches is None:
        vt = vt.reshape(nseq, vt.shape[0] // nseq, D_MODEL, vt.shape[-1])
        oa = _attn_prompt(sq(q), sq(kb), vt, lam_p, p["da_norm_w"], lam_init)
    else:
        oa = _attn_sample(sq(q), k, v, caches[0], caches[1], li, lam_p, p["da_norm_w"], lam_init)
    ob, s_new = _gla(sq(qg), sq(kg), sq(vg), sq(gk), sq(rg), s0, p["gla_norm_w"])
    T = x2d.shape[0]
    out, conv_new = _mix(oa.reshape(T, D_MODEL), ob.reshape(T, D_MODEL), ga, gb, x2d, seq_len,
                         p["w_o"], p["post_mix_w"], p["pre_ffn_w"], p["w_up"], p["conv_w"],
                         p["conv_b"], p["w_down"], conv_past, p["post_ffn_w"])
    return out, (k, v), s_new, conv_new


def kernel(x_prompt, x_sample, cache_k, cache_v, state_gla, state_conv, w_in, w_gk2, b_gk2, lambda_q1, lambda_k1, lambda_q2, lambda_k2, da_norm_w, gla_norm_w, w_o, pre_mix_w, post_mix_w, pre_ffn_w, post_ffn_w, w_up, conv_w, conv_b, w_down):
    B, S, _ = x_prompt.shape
    Bs, L, _ = x_sample.shape
    depth, _, P = cache_k.shape[:3]
    hp = x_prompt.reshape(B * S, D_MODEL)
    hs = x_sample.reshape(Bs * L, D_MODEL)
    s0_p = jnp.zeros((B, GLA_HEADS, GLA_DK, GLA_DV), F32)
    conv0_p = jnp.zeros((B, CONV_W - 1, D_FF), F32)
    c_lr = _C_LR - 2 * D_MODEL + GLA_RANK
    outs_p, outs_s = [], []
    kv_p = kv_s = None
    for li in range(depth):
        w = w_in[li]
        w_cat = jnp.concatenate(
            [w[:, :c_lr - GLA_RANK], w[:, c_lr:], w[:, c_lr - GLA_RANK:c_lr],
             jnp.zeros((D_MODEL, LANES - GLA_RANK), w.dtype)], axis=1).astype(BF16)
        p = dict(
            w_cat=w_cat,
            w_gk2=jnp.concatenate([w_gk2[li], jnp.zeros((LANES - GLA_RANK, w_gk2.shape[-1]),
                                                        w_gk2.dtype)], axis=0).astype(BF16),
            b_gk2=b_gk2[li][None, :],
            lq1=lambda_q1[li][None, :], lk1=lambda_k1[li][None, :],
            lq2=lambda_q2[li][None, :], lk2=lambda_k2[li][None, :],
            da_norm_w=da_norm_w[li][None, :], gla_norm_w=gla_norm_w[li][None, :],
            w_o=w_o[li].astype(BF16),
            pre_mix_w=pre_mix_w[li][None, :], post_mix_w=post_mix_w[li][None, :],
            pre_ffn_w=pre_ffn_w[li][None, :], post_ffn_w=post_ffn_w[li][None, :],
            w_up=w_up[li].astype(BF16), conv_w=conv_w[li], conv_b=conv_b[li][None, :],
            w_down=w_down[li].astype(BF16))
        hp, kv_p, *rest_p = _layer(hp, B, S, 0, None, li, depth, kv_p, s0_p, conv0_p, p)
        hs, kv_s, *rest_s = _layer(hs, Bs, L, P, (cache_k, cache_v), li, depth, kv_s,
                                   state_gla[li], state_conv[li], p)
        outs_p.append(rest_p)
        outs_s.append(rest_s)
    stack = lambda outs, j: jnp.stack([o[j] for o in outs])
    kv5 = lambda t, n, l: t.reshape(depth, n, l, DA_HEADS, DA_ROW)
    return (hp.reshape(B, S, D_MODEL), hs.reshape(Bs, L, D_MODEL),
            kv5(kv_p[0], B, S), kv5(kv_p[1], B, S), stack(outs_p, 0), stack(outs_p, 1),
            kv5(kv_s[0], Bs, L), kv5(kv_s[1], Bs, L), stack(outs_s, 0), stack(outs_s, 1))
```

```python
import functools
import math

import jax
import jax.numpy as jnp
from jax import lax
from jax.experimental import pallas as pl
from jax.experimental.pallas import tpu as pltpu

F32 = jnp.float32
BF16 = jnp.bfloat16

D_MODEL = 1024
CHUNK = 64
DA_HEADS = 8
DA_HEAD_DIM = 64
DA_ROW = 2 * DA_HEAD_DIM
ROPE_DIM = DA_HEAD_DIM // 4
ROPE_HALF = ROPE_DIM // 2
ROPE_THETA = 500000.0
GLA_HEADS = 4
GLA_DK = 128
GLA_DV = 256
GLA_RANK = 16
GLA_GATE_NORM = 16.0
D_FF = 2816
CONV_W = 3
EPS = 1e-6

LANES = 128
SUBLANES = 8
VMEM_LIMIT = 56 * 1024 * 1024

_C_QA, _C_KA, _C_VA = 0, 1024, 2048
_C_QG, _C_KG, _C_VG, _C_RG = 3072, 3584, 4096, 5120
_C_GA, _C_GB, _C_LR = 6144, 7168, 8192
_W_COLS = _C_LR + LANES

_Q_SCALE = DA_HEAD_DIM ** -0.5 * math.log2(math.e)

_NT = (((1,), (1,)), ((), ()))
_TN = (((0,), (0,)), ((), ()))


def _rms(x, w):
    return x * lax.rsqrt(jnp.mean(x * x, axis=-1, keepdims=True) + EPS) * w


def _params(n_axes):
    return pltpu.CompilerParams(
        dimension_semantics=("arbitrary",) * n_axes, vmem_limit_bytes=VMEM_LIMIT)


def _resident(shape):
    nd = len(shape)
    return pl.BlockSpec(shape, lambda *_: (0,) * nd, pipeline_mode=pl.Buffered(1))


def _inproj_kernel(x_ref, nw_ref, w_ref, cos_ref, sna_ref, snb_ref, wgk_ref, bgk_ref, *refs):
    (q_ref, k_ref, v_ref, vt_ref, kb_ref, qg_ref, kg_ref, vg_ref, rg_ref, gk_ref,
     ga_ref, gb_ref) = refs[-12:]
    xb = _rms(x_ref[...], nw_ref[...]).astype(BF16)

    def proj(c0, n):
        return jnp.dot(xb, w_ref[:, c0:c0 + n], preferred_element_type=F32)

    cos, sna, snb = cos_ref[...], sna_ref[...], snb_ref[...]

    def rope(y):
        return (y * cos + pltpu.roll(y, LANES - ROPE_HALF, 1) * sna
                + pltpu.roll(y, ROPE_HALF, 1) * snb)

    lr = proj(_C_LR, LANES)
    z = jnp.dot(lr.astype(BF16), wgk_ref[...], preferred_element_type=F32) + bgk_ref[...]
    gk_ref[...] = (jnp.minimum(z, 0.0) - jnp.log1p(jnp.exp(-jnp.abs(z)))) * (1.0 / GLA_GATE_NORM)
    for blk in range(2):
        yq = proj(_C_QA + blk * 512, 512)
        yk = proj(_C_KA + blk * 512, 512)
        for h in range(4):
            cols = slice(blk * 512 + h * DA_ROW, blk * 512 + (h + 1) * DA_ROW)
            q_ref[:, cols] = (rope(yq[:, h * DA_ROW:(h + 1) * DA_ROW]) * _Q_SCALE).astype(BF16)
            kr = rope(yk[:, h * DA_ROW:(h + 1) * DA_ROW])
            k_ref[:, blk * 4 + h, :] = kr
            kb_ref[:, cols] = kr.astype(BF16)
        yv = proj(_C_VA + blk * 512, 512)
        for h in range(4):
            v_ref[:, blk * 4 + h, :] = yv[:, h * DA_ROW:(h + 1) * DA_ROW]
        vt_ref[blk * 512:(blk + 1) * 512, :] = yv.T.astype(BF16)
    qg_ref[...] = (proj(_C_QG, 512) * (GLA_DK ** -0.5)).astype(BF16)
    kg_ref[...] = proj(_C_KG, 512).astype(BF16)
    for c0, dst in ((_C_VG, vg_ref), (_C_RG, rg_ref), (_C_GA, ga_ref), (_C_GB, gb_ref)):
        for blk in range(2):
            dst[:, blk * 512:(blk + 1) * 512] = proj(c0 + blk * 512, 512).astype(BF16)


def _rope_tables(pos):
    inv = ROPE_THETA ** (-jnp.arange(ROPE_HALF, dtype=F32) * 2.0 / ROPE_DIM)
    ang = pos.astype(F32)[:, None] * inv[None, :]
    cos, sin = jnp.cos(ang), jnp.sin(ang)
    n = pos.shape[0]
    rest = DA_HEAD_DIM - ROPE_DIM
    z8 = jnp.zeros((n, ROPE_HALF), F32)
    cos64 = jnp.concatenate([cos, cos, jnp.ones((n, rest), F32)], axis=-1)
    sna64 = jnp.concatenate([-sin, z8, jnp.zeros((n, rest), F32)], axis=-1)
    snb64 = jnp.concatenate([z8, sin, jnp.zeros((n, rest), F32)], axis=-1)
    return tuple(jnp.tile(t, (1, 2)) for t in (cos64, sna64, snb64))


def _inproj(x2d, seq_len, pos0, nw, w_cat, wgk, bgk, li, depth, kv_stack):
    T = x2d.shape[0]
    tm = min(512, T)
    period = max(seq_len, tm)
    pos = pos0 + jnp.arange(period, dtype=jnp.int32) % seq_len
    cos, sna, snb = _rope_tables(pos)
    nper = period // tm
    row = lambda w: pl.BlockSpec((tm, w), lambda i: (i, 0))
    tab = pl.BlockSpec((tm, LANES), lambda i: (i % nper, 0))
    stack = pl.BlockSpec((None, tm, DA_HEADS, DA_ROW), lambda i: (li, i, 0, 0))
    stack_shape = jax.ShapeDtypeStruct((depth, T, DA_HEADS, DA_ROW), F32)
    vt = pl.BlockSpec((None, D_MODEL, tm), lambda i: (i, 0, 0))
    vt_shape = jax.ShapeDtypeStruct((T // tm, D_MODEL, tm), BF16)
    widths = (1024, 1024, 512, 512, 1024, 1024, 512, 1024, 1024)
    dtypes = (BF16, BF16, BF16, BF16, BF16, BF16, F32, BF16, BF16)
    rows = [row(w) for w in widths]
    shapes = [jax.ShapeDtypeStruct((T, w), d) for w, d in zip(widths, dtypes)]
    prev = () if kv_stack is None else tuple(kv_stack)
    n_in = 8
    return pl.pallas_call(
        _inproj_kernel,
        grid=(T // tm,),
        in_specs=[row(D_MODEL), _resident((1, D_MODEL)), _resident((D_MODEL, _W_COLS)),
                  tab, tab, tab, _resident((LANES, 512)), _resident((1, 512))]
                 + [pl.BlockSpec(memory_space=pl.ANY)] * len(prev),
        out_specs=rows[:1] + [stack, stack, vt] + rows[1:],
        out_shape=shapes[:1] + [stack_shape, stack_shape, vt_shape] + shapes[1:],
        input_output_aliases={n_in + j: 1 + j for j in range(len(prev))},
        compiler_params=_params(1),
        name="inproj",
    )(x2d, nw, w_cat, cos, sna, snb, wgk, bgk, *prev)


def _split_q(q):
    lane = lax.broadcasted_iota(jnp.int32, q.shape, 1)
    zero = jnp.zeros_like(q)
    return jnp.concatenate([jnp.where(lane < DA_HEAD_DIM, q, zero),
                            jnp.where(lane < DA_HEAD_DIM, zero, q)], axis=0)


def _lambda(lq1, lk1, lq2, lk2, lam_init):
    return (jnp.exp(jnp.sum(lq1[...] * lk1[...], axis=1, keepdims=True))
            - jnp.exp(jnp.sum(lq2[...] * lk2[...], axis=1, keepdims=True)) + lam_init)


def _attn_finish(o12, lam, nw, lam_init, n):
    od = o12[:n] - lam * o12[n:]
    return (_rms(od, nw) * (1.0 - lam_init)).astype(BF16)


_ATTN_HEADS_PER_STEP = 4


_ONES_ROWS = 16
_ATTN_TILE = 512


def _attn_prompt_kernel(q_ref, k_ref, vt_ref, lq1, lk1, lq2, lk2, nwc_ref, o_ref,
                        q2_ref, sa_ref, sb_ref, m_ref, l_ref, acc_ref, *, tq, nh, lam_init):
    i = pl.program_id(2)
    vt_per_tile = tq // vt_ref.shape[-1]
    heads = [slice(h * DA_ROW, (h + 1) * DA_ROW) for h in range(nh)]
    comp = lax.broadcasted_iota(jnp.int32, (DA_ROW, tq), 0) // DA_HEAD_DIM
    for h, cols in enumerate(heads):
        qt = q_ref[:, cols].astype(F32).T
        q2_ref[h] = jnp.concatenate([jnp.where(comp == 0, qt, 0.0),
                                     jnp.where(comp == 0, 0.0, qt)], axis=1).astype(BF16)
    m_ref[...] = jnp.full(m_ref.shape, -jnp.inf, F32)
    l_ref[...] = jnp.zeros(l_ref.shape, F32)
    acc_ref[...] = jnp.zeros(acc_ref.shape, F32)
    ones = jnp.ones((_ONES_ROWS, tq), BF16)

    def scores(j, dst):
        r = pl.multiple_of(j * tq, tq)
        for h, cols in enumerate(heads):
            dst[h] = jnp.dot(k_ref[pl.ds(r, tq), cols], q2_ref[h],
                             preferred_element_type=F32)

    def consume(j, src, masked):
        for h, cols in enumerate(heads):
            s = src[h]
            if masked:
                key = lax.broadcasted_iota(jnp.int32, s.shape, 0)
                qry = lax.broadcasted_iota(jnp.int32, s.shape, 1)
                s = jnp.where((key // CHUNK) <= ((qry % tq) // CHUNK), s, -jnp.inf)
            m_prev = m_ref[h]
            m_new = jnp.maximum(m_prev, jnp.max(s, axis=0, keepdims=True))
            alpha = jnp.exp2(m_prev - m_new)
            p = jnp.exp2(s - m_new)
            vtj = jnp.concatenate([vt_ref[j * vt_per_tile + u, cols, :]
                                   for u in range(vt_per_tile)], axis=1)
            vte = jnp.concatenate([vtj, ones], axis=0)
            pv = jnp.dot(vte, p.astype(BF16), preferred_element_type=F32)
            acc_ref[h] = alpha * acc_ref[h] + pv[:DA_ROW]
            l_ref[h] = alpha * l_ref[h] + pv[DA_ROW:DA_ROW + 1]
            m_ref[h] = m_new

    scores(0, sa_ref)

    def pair(t, carry):
        j = 2 * t
        scores(j + 1, sb_ref)
        consume(j, sa_ref, False)
        scores(j + 2, sa_ref)
        consume(j + 1, sb_ref, False)
        return carry
    lax.fori_loop(0, lax.shift_right_logical(i, 1), pair, 0)

    @pl.when(lax.bitwise_and(i, 1) == 0)
    def _():
        consume(i, sa_ref, True)

    @pl.when(lax.bitwise_and(i, 1) == 1)
    def _():
        scores(i, sb_ref)
        consume(i - 1, sa_ref, False)
        consume(i, sb_ref, True)

    lam = _lambda(lq1, lk1, lq2, lk2, lam_init)
    for h in range(nh):
        o12 = acc_ref[h] / l_ref[h]
        od = o12[:, :tq] - lam * o12[:, tq:]
        y = od * lax.rsqrt(jnp.mean(od * od, axis=0, keepdims=True) + EPS) * nwc_ref[...]
        o_ref[:, h * DA_ROW:(h + 1) * DA_ROW] = (y * (1.0 - lam_init)).T.astype(BF16)


def _attn_prompt(q, k, vt, lam_p, nw, lam_init):
    B, S, _ = q.shape
    n_vt, tv = vt.shape[1], vt.shape[-1]
    tq = _ATTN_TILE if (S % _ATTN_TILE == 0 and _ATTN_TILE % tv == 0) else tv
    nh = _ATTN_HEADS_PER_STEP
    w = nh * DA_ROW
    lam_spec = _resident((1, DA_HEAD_DIM))
    stat = pltpu.VMEM((nh, 1, 2 * tq), F32)
    once = pl.Buffered(1)
    return pl.pallas_call(
        functools.partial(_attn_prompt_kernel, tq=tq, nh=nh, lam_init=lam_init),
        grid=(B, DA_HEADS // nh, S // tq),
        in_specs=[pl.BlockSpec((None, tq, w), lambda b, h, i: (b, i, h)),
                  pl.BlockSpec((None, S, w), lambda b, h, i: (b, 0, h)),
                  pl.BlockSpec((None, n_vt, w, tv), lambda b, h, i: (b, 0, h, 0),
                               pipeline_mode=once),
                  lam_spec, lam_spec, lam_spec, lam_spec, _resident((DA_ROW, 1))],
        out_specs=pl.BlockSpec((None, tq, w), lambda b, h, i: (b, i, h)),
        out_shape=jax.ShapeDtypeStruct((B, S, D_MODEL), BF16),
        scratch_shapes=[pltpu.VMEM((nh, DA_ROW, 2 * tq), BF16),
                        pltpu.VMEM((nh, tq, 2 * tq), F32), pltpu.VMEM((nh, tq, 2 * tq), F32),
                        stat, stat, pltpu.VMEM((nh, DA_ROW, 2 * tq), F32)],
        compiler_params=_params(3),
        name="attn_prompt",
    )(q, k, vt, *lam_p, nw.reshape(DA_ROW, 1))


_SAMPLE_KV_ROWS = 2048


def _attn_sample_kernel(q_ref, kn_ref, vn_ref, ck_ref, cv_ref, lq1, lk1, lq2, lk2, nw_ref, o_ref,
                        ckb_ref, cvb_ref, *, lam_init):
    L, P = q_ref.shape[0], ck_ref.shape[0]
    n_past = P * DA_HEADS
    ch = min(_SAMPLE_KV_ROWS, n_past)
    rows = 2 * L * DA_HEADS
    flat = lambda r: r[...].reshape(r.shape[0] * DA_HEADS, DA_ROW).astype(BF16)
    ckb_ref[...] = flat(ck_ref)
    cvb_ref[...] = flat(cv_ref)
    kn, vn = flat(kn_ref), flat(vn_ref)
    q_all = jnp.concatenate([_split_q(q_ref[:, h * DA_ROW:(h + 1) * DA_ROW])
                             for h in range(DA_HEADS)], axis=0)

    def own_head(n_cols):
        row_head = lax.broadcasted_iota(jnp.int32, (rows, n_cols), 0) // (2 * L)
        col_head = lax.broadcasted_iota(jnp.int32, (rows, n_cols), 1) % DA_HEADS
        return row_head == col_head

    def update(carry, kc, vc, visible):
        m, l, acc = carry
        s = lax.dot_general(q_all, kc, _NT, preferred_element_type=F32)
        s = jnp.where(visible, s, -jnp.inf)
        m_new = jnp.maximum(m, jnp.max(s, axis=1, keepdims=True))
        alpha = jnp.exp2(m - m_new)
        p = jnp.exp2(s - m_new)
        return (m_new, alpha * l + jnp.sum(p, axis=1, keepdims=True),
                alpha * acc + jnp.dot(p.astype(BF16), vc, preferred_element_type=F32))

    vis_p = own_head(ch)

    def chunk(c, carry):
        r = pl.multiple_of(c * ch, ch)
        return update(carry, ckb_ref[pl.ds(r, ch), :], cvb_ref[pl.ds(r, ch), :], vis_p)

    init = (jnp.full((rows, 1), -jnp.inf, F32), jnp.zeros((rows, 1), F32),
            jnp.zeros((rows, DA_ROW), F32))
    carry = lax.fori_loop(0, n_past // ch, chunk, init)
    _, l, acc = update(carry, kn, vn, own_head(L * DA_HEADS))
    o12 = acc / l
    lam = _lambda(lq1, lk1, lq2, lk2, lam_init)
    for h in range(DA_HEADS):
        o_ref[:, h * DA_ROW:(h + 1) * DA_ROW] = _attn_finish(
            o12[h * 2 * L:(h + 1) * 2 * L], lam, nw_ref[...], lam_init, L)


def _attn_sample(q, k_stack, v_stack, cache_k, cache_v, li, lam_p, nw, lam_init):
    B, L, _ = q.shape
    P = cache_k.shape[2]
    row = pl.BlockSpec((None, L, D_MODEL), lambda b: (b, 0, 0))
    new = pl.BlockSpec((None, L, DA_HEADS, DA_ROW), lambda b: (li, b, 0, 0))
    past = pl.BlockSpec((None, None, P, DA_HEADS, DA_ROW), lambda b: (li, b, 0, 0, 0))
    lam_spec = _resident((1, DA_HEAD_DIM))
    return pl.pallas_call(
        functools.partial(_attn_sample_kernel, lam_init=lam_init),
        grid=(B,),
        in_specs=[row, new, new, past, past, lam_spec, lam_spec, lam_spec, lam_spec,
                  _resident((1, DA_ROW))],
        out_specs=row,
        out_shape=jax.ShapeDtypeStruct((B, L, D_MODEL), BF16),
        scratch_shapes=[pltpu.VMEM((P * DA_HEADS, DA_ROW), BF16)] * 2,
        compiler_params=_params(1),
        name="attn_sample",
    )(q, k_stack, v_stack, cache_k, cache_v, *lam_p, nw)


def _gla_kernel(q_ref, k_ref, v_ref, g_ref, r_ref, s0_ref, nw_ref, o_ref, sout_ref,
                qs_ref, ks_ref, vs_ref, os_ref, S_ref, *, C, nchunk):
    A = C // SUBLANES
    i = pl.program_id(2)

    @pl.when(i == 0)
    def _():
        S_ref[...] = s0_ref[...]

    qs_ref[...] = q_ref[...].astype(F32)
    ks_ref[...] = k_ref[...].astype(F32)
    for half in range(GLA_DV // LANES):
        vs_ref[half] = v_ref[:, half * LANES:(half + 1) * LANES].astype(F32)

    def time_of(r):
        return (r % SUBLANES) * A + r // SUBLANES

    t_r = time_of(lax.broadcasted_iota(jnp.int32, (C, C), 0))
    t_c = time_of(lax.broadcasted_iota(jnp.int32, (C, C), 1))
    t_k = time_of(lax.broadcasted_iota(jnp.int32, (C, GLA_DK), 0))
    halves = []
    hsz = C // 2
    while hsz >= A:
        halves.append(hsz)
        hsz //= 2
    sel = [t_c <= t_r] + [t_c <= (t_r // (2 * h)) * (2 * h) + h - 1 for h in halves]
    sel = jnp.concatenate([jnp.where(m, 1.0, 0.0) for m in sel], axis=0).astype(BF16)
    same_parent = [(t_r // (2 * h)) == (t_c // (2 * h)) for h in halves]
    second_half = [(t_k // h) % 2 == 1 for h in halves]
    r8 = lax.broadcasted_iota(jnp.int32, (SUBLANES, C), 0)
    c8 = lax.broadcasted_iota(jnp.int32, (SUBLANES, C), 1)
    ones_cv = jnp.ones((C, GLA_DV), BF16)

    def load(ref, base):
        return jnp.concatenate(
            [ref[pl.ds(base + a, SUBLANES, stride=A), :] for a in range(A)], axis=0)

    chunks = range(nchunk)
    q = [load(qs_ref, c * C) for c in chunks]
    k = [load(ks_ref, c * C) for c in chunks]
    g = [load(g_ref, c * C) for c in chunks]
    v = [jnp.concatenate([load(vs_ref.at[half], c * C) for half in range(GLA_DV // LANES)],
                         axis=1).astype(BF16) for c in chunks]
    g_hi = [x.astype(BF16) for x in g]
    g_lo = [(x - hi.astype(F32)).astype(BF16) for x, hi in zip(g, g_hi)]
    sums = [jnp.dot(sel, jnp.concatenate([hi, lo], axis=1), preferred_element_type=F32)
            for hi, lo in zip(g_hi, g_lo)]
    sums = [x[:, :GLA_DK] + x[:, GLA_DK:] for x in sums]
    b = [x[:C] for x in sums]
    decay = [lax.dot_general(hi, ones_cv, _TN, preferred_element_type=F32)
             + lax.dot_general(lo, ones_cv, _TN, preferred_element_type=F32)
             for hi, lo in zip(g_hi, g_lo)]

    amat = []
    for c in chunks:
        am = jnp.zeros((C, C), F32)
        for lvl, h in enumerate(halves):
            x = b[c] - sums[c][(lvl + 1) * C:(lvl + 2) * C]
            sec = second_half[lvl]
            qh = jnp.where(sec, q[c] * jnp.exp(jnp.where(sec, x, 0.0)), 0.0).astype(BF16)
            kh = jnp.where(sec, 0.0, k[c] * jnp.exp(jnp.where(sec, 0.0, -x))).astype(BF16)
            al = lax.dot_general(qh, kh, _NT, preferred_element_type=F32)
            am = am + jnp.where(same_parent[lvl], al, 0.0)
        amat.append(am)
    for c in chunks:
        diag = []
        for a in range(A):
            ra = slice(a * SUBLANES, (a + 1) * SUBLANES)
            acc = jnp.zeros((SUBLANES, C), F32)
            for a2 in range(a + 1):
                rb = slice(a2 * SUBLANES, (a2 + 1) * SUBLANES)
                e = q[c][ra] * k[c][rb] * jnp.exp(b[c][ra] - b[c][rb])
                acc = acc + jnp.where(c8 == a2 * SUBLANES + r8,
                                      jnp.sum(e, axis=1, keepdims=True), 0.0)
            diag.append(acc)
        amat[c] = amat[c] + jnp.concatenate(diag, axis=0)
    o_intra = [jnp.dot(amat[c].astype(BF16), v[c], preferred_element_type=F32) for c in chunks]
    upd = [lax.dot_general((k[c] * jnp.exp(b[c][C - 1:C] - b[c])).astype(BF16), v[c], _TN,
                           preferred_element_type=F32) for c in chunks]
    q_dec = [(q[c] * jnp.exp(b[c])).astype(BF16) for c in chunks]

    S = S_ref[...]
    for c in chunks:
        o = o_intra[c] + jnp.dot(q_dec[c], S.astype(BF16), preferred_element_type=F32)
        S = jnp.exp(decay[c]) * S + upd[c]
        for a in range(A):
            for half in range(GLA_DV // LANES):
                os_ref[half, pl.ds(c * C + a, SUBLANES, stride=A), :] = (
                    o[a * SUBLANES:(a + 1) * SUBLANES, half * LANES:(half + 1) * LANES])
    S_ref[...] = S

    og = jnp.concatenate([os_ref[half] for half in range(GLA_DV // LANES)], axis=1)
    y = _rms(og, nw_ref[...])
    r = r_ref[...].astype(F32)
    o_ref[...] = (y * (r * jax.nn.sigmoid(r))).astype(BF16)

    @pl.when(i == pl.num_programs(2) - 1)
    def _():
        sout_ref[...] = S_ref[...]


def _gla(qg, kg, vg, gk, rg, s0, nw):
    B, L, _ = qg.shape
    C = CHUNK if L % CHUNK == 0 else L
    tc = min(512, L)
    dk = lambda b, h, i: (b, i, h)
    st = pl.BlockSpec((None, None, GLA_DK, GLA_DV), lambda b, h, i: (b, h, 0, 0))
    return pl.pallas_call(
        functools.partial(_gla_kernel, C=C, nchunk=tc // C),
        grid=(B, GLA_HEADS, L // tc),
        in_specs=[pl.BlockSpec((None, tc, GLA_DK), dk), pl.BlockSpec((None, tc, GLA_DK), dk),
                  pl.BlockSpec((None, tc, GLA_DV), dk), pl.BlockSpec((None, tc, GLA_DK), dk),
                  pl.BlockSpec((None, tc, GLA_DV), dk), st, _resident((1, GLA_DV))],
        out_specs=[pl.BlockSpec((None, tc, GLA_DV), dk), st],
        out_shape=[jax.ShapeDtypeStruct((B, L, D_MODEL), BF16),
                   jax.ShapeDtypeStruct((B, GLA_HEADS, GLA_DK, GLA_DV), F32)],
        scratch_shapes=[pltpu.VMEM((tc, GLA_DK), F32), pltpu.VMEM((tc, GLA_DK), F32),
                        pltpu.VMEM((GLA_DV // LANES, tc, LANES), F32),
                        pltpu.VMEM((GLA_DV // LANES, tc, LANES), F32),
                        pltpu.VMEM((GLA_DK, GLA_DV), F32)],
        compiler_params=_params(3),
        name="gla",
    )(qg, kg, vg, gk, rg, s0, nw)


_FF_BLK = 256
_MIX_ROWS = 512
_MIX_ROWS_SHORT = 256


def _mix_kernel(oa_ref, ob_ref, ga_ref, gb_ref, x_ref, wo_ref, nw0_ref,
                nw1_ref, wup_ref, cw_ref, cb_ref, wdn_ref, past_ref, nw2_ref,
                out_ref, cs_ref, act_ref, carry_ref, *, tiles_per_seq, seq_len):
    i = pl.program_id(0)
    tm = x_ref.shape[0]
    whole = tiles_per_seq == 0

    if not whole:
        @pl.when(i % tiles_per_seq == 0)
        def _():
            carry_ref[SUBLANES - 2:SUBLANES, :] = past_ref[...]

    f = lambda r: r[...].astype(F32)
    merged = jax.nn.sigmoid(f(ga_ref)) * f(oa_ref) + jax.nn.sigmoid(f(gb_ref)) * f(ob_ref)
    y = jnp.dot(merged.astype(BF16), wo_ref[...], preferred_element_type=F32)
    h = x_ref[...] + _rms(y, nw0_ref[...])
    hb = _rms(h, nw1_ref[...]).astype(BF16)
    rowid = lax.broadcasted_iota(jnp.int32, (tm, _FF_BLK), 0)
    for blk in range(D_FF // _FF_BLK):
        cols = slice(blk * _FF_BLK, (blk + 1) * _FF_BLK)
        u = jnp.dot(hb, wup_ref[:, cols], preferred_element_type=F32)
        g = jnp.dot(hb, wup_ref[:, D_FF + blk * _FF_BLK:D_FF + (blk + 1) * _FF_BLK],
                    preferred_element_type=F32)
        if whole:
            pos = rowid % seq_len
            prev2, prev1 = past_ref[0, :, cols], past_ref[1, :, cols]
        else:
            pos = rowid
            prev2 = carry_ref[SUBLANES - 2:SUBLANES - 1, cols]
            prev1 = carry_ref[SUBLANES - 1:SUBLANES, cols]
        g1 = jnp.where(pos == 0, prev1, pltpu.roll(g, 1, 0))
        g2 = jnp.where(pos == 0, prev2, jnp.where(pos == 1, prev1, pltpu.roll(g, 2, 0)))
        gc = cb_ref[:, cols] + cw_ref[0:1, cols] * g2 + cw_ref[1:2, cols] * g1 + cw_ref[2:3, cols] * g
        act_ref[:, cols] = (jax.nn.gelu(gc, approximate=True) * u).astype(BF16)
        if whole:
            last = g.reshape(tm // seq_len, seq_len, _FF_BLK)[:, seq_len - (CONV_W - 1):, :]
            cs_ref[:, :, cols] = last
        else:
            carry_ref[:, cols] = g[tm - SUBLANES:tm, :]
            cs_ref[:, cols] = g[tm - (CONV_W - 1):tm, :]
    ffn = jnp.dot(act_ref[...], wdn_ref[...], preferred_element_type=F32)
    out_ref[...] = h + _rms(ffn, nw2_ref[...])


def _mix(oa, ob, ga, gb, x2d, seq_len, wo, nw0, nw1, wup, cw, cb, wdn, conv_past, nw2):
    T = x2d.shape[0]
    nseq = T // seq_len
    if seq_len >= _MIX_ROWS:
        tm = _MIX_ROWS
        tps = seq_len // tm
        past = conv_past
        past_spec = st = pl.BlockSpec((None, CONV_W - 1, D_FF), lambda i: (i // tps, 0, 0))
    else:
        tm = min(_MIX_ROWS_SHORT, T)
        tps = 0
        past = jnp.repeat(jnp.swapaxes(conv_past, 0, 1), seq_len, axis=1)
        past_spec = pl.BlockSpec((CONV_W - 1, tm, D_FF), lambda i: (0, i, 0))
        st = pl.BlockSpec((tm // seq_len, CONV_W - 1, D_FF), lambda i: (i, 0, 0))
    row = pl.BlockSpec((tm, D_MODEL), lambda i: (i, 0))
    vec = _resident((1, D_MODEL))
    return pl.pallas_call(
        functools.partial(_mix_kernel, tiles_per_seq=tps, seq_len=seq_len),
        grid=(T // tm,),
        in_specs=[row, row, row, row, row, _resident((D_MODEL, D_MODEL)), vec,
                  vec, _resident((D_MODEL, 2 * D_FF)),
                  _resident((CONV_W, D_FF)), _resident((1, D_FF)), _resident((D_FF, D_MODEL)),
                  past_spec, vec],
        out_specs=[row, st],
        out_shape=[jax.ShapeDtypeStruct((T, D_MODEL), F32),
                   jax.ShapeDtypeStruct((nseq, CONV_W - 1, D_FF), F32)],
        scratch_shapes=[pltpu.VMEM((tm, D_FF), BF16), pltpu.VMEM((SUBLANES, D_FF), F32)],
        compiler_params=_params(1),
        name="mix",
    )(oa, ob, ga, gb, x2d, wo, nw0, nw1, wup, cw, cb, wdn, past, nw2)


def _layer(x2d, nseq, seq_len, pos0, caches, li, depth, kv_stack, s0, conv_past, p):
    (q, k, v, vt, kb, qg, kg, vg, rg, gk, ga, gb) = _inproj(
        x2d, seq_len, pos0, p["pre_mix_w"], p["w_cat"], p["w_gk2"], p["b_gk2"], li, depth,
        kv_stack)
    lam_init = 0.8 - 0.6 * math.exp(-0.3 * li)
    lam_p = (p["lq1"], p["lk1"], p["lq2"], p["lk2"])
    sq = lambda t: t.reshape(nseq, seq_len, t.shape[-1])
    if caches is None:
        vt = vt.reshape(nseq, vt.shape[0] // nseq, D_MODEL, vt.shape[-1])
        oa = _attn_prompt(sq(q), sq(kb), vt, lam_p, p["da_norm_w"], lam_init)
    else:
        oa = _attn_sample(sq(q), k, v, caches[0], caches[1], li, lam_p, p["da_norm_w"], lam_init)
    ob, s_new = _gla(sq(qg), sq(kg), sq(vg), sq(gk), sq(rg), s0, p["gla_norm_w"])
    T = x2d.shape[0]
    out, conv_new = _mix(oa.reshape(T, D_MODEL), ob.reshape(T, D_MODEL), ga, gb, x2d, seq_len,
                         p["w_o"], p["post_mix_w"], p["pre_ffn_w"], p["w_up"], p["conv_w"],
                         p["conv_b"], p["w_down"], conv_past, p["post_ffn_w"])
    return out, (k, v), s_new, conv_new


def kernel(x_prompt, x_sample, cache_k, cache_v, state_gla, state_conv, w_in, w_gk2, b_gk2, lambda_q1, lambda_k1, lambda_q2, lambda_k2, da_norm_w, gla_norm_w, w_o, pre_mix_w, post_mix_w, pre_ffn_w, post_ffn_w, w_up, conv_w, conv_b, w_down):
    B, S, _ = x_prompt.shape
    Bs, L, _ = x_sample.shape
    depth, _, P = cache_k.shape[:3]
    hp = x_prompt.reshape(B * S, D_MODEL)
    hs = x_sample.reshape(Bs * L, D_MODEL)
    s0_p = jnp.zeros((B, GLA_HEADS, GLA_DK, GLA_DV), F32)
    conv0_p = jnp.zeros((B, CONV_W - 1, D_FF), F32)
    c_lr = _C_LR - 2 * D_MODEL + GLA_RANK
    outs_p, outs_s = [], []
    kv_p = kv_s = None
    for li in range(depth):
        w = w_in[li]
        w_cat = jnp.concatenate(
            [w[:, :c_lr - GLA_RANK], w[:, c_lr:], w[:, c_lr - GLA_RANK:c_lr],
             jnp.zeros((D_MODEL, LANES - GLA_RANK), w.dtype)], axis=1).astype(BF16)
        p = dict(
            w_cat=w_cat,
            w_gk2=jnp.concatenate([w_gk2[li], jnp.zeros((LANES - GLA_RANK, w_gk2.shape[-1]),
                                                        w_gk2.dtype)], axis=0).astype(BF16),
            b_gk2=b_gk2[li][None, :],
            lq1=lambda_q1[li][None, :], lk1=lambda_k1[li][None, :],
            lq2=lambda_q2[li][None, :], lk2=lambda_k2[li][None, :],
            da_norm_w=da_norm_w[li][None, :], gla_norm_w=gla_norm_w[li][None, :],
            w_o=w_o[li].astype(BF16),
            pre_mix_w=pre_mix_w[li][None, :], post_mix_w=post_mix_w[li][None, :],
            pre_ffn_w=pre_ffn_w[li][None, :], post_ffn_w=post_ffn_w[li][None, :],
            w_up=w_up[li].astype(BF16), conv_w=conv_w[li], conv_b=conv_b[li][None, :],
            w_down=w_down[li].astype(BF16))
        hp, kv_p, *rest_p = _layer(hp, B, S, 0, None, li, depth, kv_p, s0_p, conv0_p, p)
        hs, kv_s, *rest_s = _layer(hs, Bs, L, P, (cache_k, cache_v), li, depth, kv_s,
                                   state_gla[li], state_conv[li], p)
        outs_p.append(rest_p)
        outs_s.append(rest_s)
    stack = lambda outs, j: jnp.stack([o[j] for o in outs])
    kv5 = lambda t, n, l: t.reshape(depth, n, l, DA_HEADS, DA_ROW)
    return (hp.reshape(B, S, D_MODEL), hs.reshape(Bs, L, D_MODEL),
            kv5(kv_p[0], B, S), kv5(kv_p[1], B, S), stack(outs_p, 0), stack(outs_p, 1),
            kv5(kv_s[0], Bs, L), kv5(kv_s[1], Bs, L), stack(outs_s, 0), stack(outs_s, 1))
```

```python
import functools
import math

import jax
import jax.numpy as jnp
from jax import lax
from jax.experimental import pallas as pl
from jax.experimental.pallas import tpu as pltpu

F32 = jnp.float32
BF16 = jnp.bfloat16

D_MODEL = 1024
CHUNK = 64
DA_HEADS = 8
DA_HEAD_DIM = 64
DA_ROW = 2 * DA_HEAD_DIM
ROPE_DIM = DA_HEAD_DIM // 4
ROPE_HALF = ROPE_DIM // 2
ROPE_THETA = 500000.0
GLA_HEADS = 4
GLA_DK = 128
GLA_DV = 256
GLA_RANK = 16
GLA_GATE_NORM = 16.0
D_FF = 2816
CONV_W = 3
EPS = 1e-6

LANES = 128
SUBLANES = 8
VMEM_LIMIT = 56 * 1024 * 1024

_C_QA, _C_KA, _C_VA = 0, 1024, 2048
_C_QG, _C_KG, _C_VG, _C_RG = 3072, 3584, 4096, 5120
_C_GA, _C_GB, _C_LR = 6144, 7168, 8192
_W_COLS = _C_LR + LANES

_Q_SCALE = DA_HEAD_DIM ** -0.5 * math.log2(math.e)

_NT = (((1,), (1,)), ((), ()))
_TN = (((0,), (0,)), ((), ()))


def _rms(x, w):
    return x * lax.rsqrt(jnp.mean(x * x, axis=-1, keepdims=True) + EPS) * w


def _params(n_axes):
    return pltpu.CompilerParams(
        dimension_semantics=("arbitrary",) * n_axes, vmem_limit_bytes=VMEM_LIMIT)


def _resident(shape):
    nd = len(shape)
    return pl.BlockSpec(shape, lambda *_: (0,) * nd, pipeline_mode=pl.Buffered(1))


def _inproj_kernel(x_ref, nw_ref, w_ref, cos_ref, sna_ref, snb_ref, wgk_ref, bgk_ref, *refs):
    (q_ref, k_ref, v_ref, vt_ref, kb_ref, qg_ref, kg_ref, vg_ref, rg_ref, gk_ref,
     ga_ref, gb_ref) = refs[-12:]
    xb = _rms(x_ref[...], nw_ref[...]).astype(BF16)

    def proj(c0, n):
        return jnp.dot(xb, w_ref[:, c0:c0 + n], preferred_element_type=F32)

    cos, sna, snb = cos_ref[...], sna_ref[...], snb_ref[...]

    def rope(y):
        return (y * cos + pltpu.roll(y, LANES - ROPE_HALF, 1) * sna
                + pltpu.roll(y, ROPE_HALF, 1) * snb)

    lr = proj(_C_LR, LANES)
    z = jnp.dot(lr.astype(BF16), wgk_ref[...], preferred_element_type=F32) + bgk_ref[...]
    gk_ref[...] = (jnp.minimum(z, 0.0) - jnp.log1p(jnp.exp(-jnp.abs(z)))) * (1.0 / GLA_GATE_NORM)
    for blk in range(2):
        yq = proj(_C_QA + blk * 512, 512)
        yk = proj(_C_KA + blk * 512, 512)
        for h in range(4):
            cols = slice(blk * 512 + h * DA_ROW, blk * 512 + (h + 1) * DA_ROW)
            q_ref[:, cols] = (rope(yq[:, h * DA_ROW:(h + 1) * DA_ROW]) * _Q_SCALE).astype(BF16)
            kr = rope(yk[:, h * DA_ROW:(h + 1) * DA_ROW])
            k_ref[:, blk * 4 + h, :] = kr
            kb_ref[:, cols] = kr.astype(BF16)
        yv = proj(_C_VA + blk * 512, 512)
        for h in range(4):
            v_ref[:, blk * 4 + h, :] = yv[:, h * DA_ROW:(h + 1) * DA_ROW]
        vt_ref[blk * 512:(blk + 1) * 512, :] = yv.T.astype(BF16)
    qg_ref[...] = (proj(_C_QG, 512) * (GLA_DK ** -0.5)).astype(BF16)
    kg_ref[...] = proj(_C_KG, 512).astype(BF16)
    for c0, dst in ((_C_VG, vg_ref), (_C_RG, rg_ref), (_C_GA, ga_ref), (_C_GB, gb_ref)):
        for blk in range(2):
            dst[:, blk * 512:(blk + 1) * 512] = proj(c0 + blk * 512, 512).astype(BF16)


def _rope_tables(pos):
    inv = ROPE_THETA ** (-jnp.arange(ROPE_HALF, dtype=F32) * 2.0 / ROPE_DIM)
    ang = pos.astype(F32)[:, None] * inv[None, :]
    cos, sin = jnp.cos(ang), jnp.sin(ang)
    n = pos.shape[0]
    rest = DA_HEAD_DIM - ROPE_DIM
    z8 = jnp.zeros((n, ROPE_HALF), F32)
    cos64 = jnp.concatenate([cos, cos, jnp.ones((n, rest), F32)], axis=-1)
    sna64 = jnp.concatenate([-sin, z8, jnp.zeros((n, rest), F32)], axis=-1)
    snb64 = jnp.concatenate([z8, sin, jnp.zeros((n, rest), F32)], axis=-1)
    return tuple(jnp.tile(t, (1, 2)) for t in (cos64, sna64, snb64))


def _inproj(x2d, seq_len, pos0, nw, w_cat, wgk, bgk, li, depth, kv_stack):
    T = x2d.shape[0]
    tm = min(512, T)
    period = max(seq_len, tm)
    pos = pos0 + jnp.arange(period, dtype=jnp.int32) % seq_len
    cos, sna, snb = _rope_tables(pos)
    nper = period // tm
    row = lambda w: pl.BlockSpec((tm, w), lambda i: (i, 0))
    tab = pl.BlockSpec((tm, LANES), lambda i: (i % nper, 0))
    stack = pl.BlockSpec((None, tm, DA_HEADS, DA_ROW), lambda i: (li, i, 0, 0))
    stack_shape = jax.ShapeDtypeStruct((depth, T, DA_HEADS, DA_ROW), F32)
    vt = pl.BlockSpec((None, D_MODEL, tm), lambda i: (i, 0, 0))
    vt_shape = jax.ShapeDtypeStruct((T // tm, D_MODEL, tm), BF16)
    widths = (1024, 1024, 512, 512, 1024, 1024, 512, 1024, 1024)
    dtypes = (BF16, BF16, BF16, BF16, BF16, BF16, F32, BF16, BF16)
    rows = [row(w) for w in widths]
    shapes = [jax.ShapeDtypeStruct((T, w), d) for w, d in zip(widths, dtypes)]
    prev = () if kv_stack is None else tuple(kv_stack)
    n_in = 8
    return pl.pallas_call(
        _inproj_kernel,
        grid=(T // tm,),
        in_specs=[row(D_MODEL), _resident((1, D_MODEL)), _resident((D_MODEL, _W_COLS)),
                  tab, tab, tab, _resident((LANES, 512)), _resident((1, 512))]
                 + [pl.BlockSpec(memory_space=pl.ANY)] * len(prev),
        out_specs=rows[:1] + [stack, stack, vt] + rows[1:],
        out_shape=shapes[:1] + [stack_shape, stack_shape, vt_shape] + shapes[1:],
        input_output_aliases={n_in + j: 1 + j for j in range(len(prev))},
        compiler_params=_params(1),
        name="inproj",
    )(x2d, nw, w_cat, cos, sna, snb, wgk, bgk, *prev)


def _split_q(q):
    lane = lax.broadcasted_iota(jnp.int32, q.shape, 1)
    zero = jnp.zeros_like(q)
    return jnp.concatenate([jnp.where(lane < DA_HEAD_DIM, q, zero),
                            jnp.where(lane < DA_HEAD_DIM, zero, q)], axis=0)


def _lambda(lq1, lk1, lq2, lk2, lam_init):
    return (jnp.exp(jnp.sum(lq1[...] * lk1[...], axis=1, keepdims=True))
            - jnp.exp(jnp.sum(lq2[...] * lk2[...], axis=1, keepdims=True)) + lam_init)


def _attn_finish(o12, lam, nw, lam_init, n):
    od = o12[:n] - lam * o12[n:]
    return (_rms(od, nw) * (1.0 - lam_init)).astype(BF16)


_ATTN_HEADS_PER_STEP = 4


_ONES_ROWS = 16
_ATTN_TILE = 512


def _attn_prompt_kernel(q_ref, k_ref, vt_ref, lq1, lk1, lq2, lk2, nwc_ref, o_ref,
                        q2_ref, sa_ref, sb_ref, m_ref, l_ref, acc_ref, *, tq, nh, lam_init):
    i = pl.program_id(2)
    vt_per_tile = tq // vt_ref.shape[-1]
    heads = [slice(h * DA_ROW, (h + 1) * DA_ROW) for h in range(nh)]
    comp = lax.broadcasted_iota(jnp.int32, (DA_ROW, tq), 0) // DA_HEAD_DIM
    for h, cols in enumerate(heads):
        qt = q_ref[:, cols].astype(F32).T
        q2_ref[h] = jnp.concatenate([jnp.where(comp == 0, qt, 0.0),
                                     jnp.where(comp == 0, 0.0, qt)], axis=1).astype(BF16)
    m_ref[...] = jnp.full(m_ref.shape, -jnp.inf, F32)
    l_ref[...] = jnp.zeros(l_ref.shape, F32)
    acc_ref[...] = jnp.zeros(acc_ref.shape, F32)
    ones = jnp.ones((_ONES_ROWS, tq), BF16)

    def scores(j, dst):
        r = pl.multiple_of(j * tq, tq)
        for h, cols in enumerate(heads):
            dst[h] = jnp.dot(k_ref[pl.ds(r, tq), cols], q2_ref[h],
                             preferred_element_type=F32)

    def consume(j, src, masked):
        for h, cols in enumerate(heads):
            s = src[h]
            if masked:
                key = lax.broadcasted_iota(jnp.int32, s.shape, 0)
                qry = lax.broadcasted_iota(jnp.int32, s.shape, 1)
                s = jnp.where((key // CHUNK) <= ((qry % tq) // CHUNK), s, -jnp.inf)
            m_prev = m_ref[h]
            m_new = jnp.maximum(m_prev, jnp.max(s, axis=0, keepdims=True))
            alpha = jnp.exp2(m_prev - m_new)
            p = jnp.exp2(s - m_new)
            vtj = jnp.concatenate([vt_ref[j * vt_per_tile + u, cols, :]
                                   for u in range(vt_per_tile)], axis=1)
            vte = jnp.concatenate([vtj, ones], axis=0)
            pv = jnp.dot(vte, p.astype(BF16), preferred_element_type=F32)
            acc_ref[h] = alpha * acc_ref[h] + pv[:DA_ROW]
            l_ref[h] = alpha * l_ref[h] + pv[DA_ROW:DA_ROW + 1]
            m_ref[h] = m_new

    scores(0, sa_ref)

    def pair(t, carry):
        j = 2 * t
        scores(j + 1, sb_ref)
        consume(j, sa_ref, False)
        scores(j + 2, sa_ref)
        consume(j + 1, sb_ref, False)
        return carry
    lax.fori_loop(0, lax.shift_right_logical(i, 1), pair, 0)

    @pl.when(lax.bitwise_and(i, 1) == 0)
    def _():
        consume(i, sa_ref, True)

    @pl.when(lax.bitwise_and(i, 1) == 1)
    def _():
        scores(i, sb_ref)
        consume(i - 1, sa_ref, False)
        consume(i, sb_ref, True)

    lam = _lambda(lq1, lk1, lq2, lk2, lam_init)
    for h in range(nh):
        o12 = acc_ref[h] / l_ref[h]
        od = o12[:, :tq] - lam * o12[:, tq:]
        y = od * lax.rsqrt(jnp.mean(od * od, axis=0, keepdims=True) + EPS) * nwc_ref[...]
        o_ref[:, h * DA_ROW:(h + 1) * DA_ROW] = (y * (1.0 - lam_init)).T.astype(BF16)


def _attn_prompt(q, k, vt, lam_p, nw, lam_init):
    B, S, _ = q.shape
    n_vt, tv = vt.shape[1], vt.shape[-1]
    tq = _ATTN_TILE if (S % _ATTN_TILE == 0 and _ATTN_TILE % tv == 0) else tv
    nh = _ATTN_HEADS_PER_STEP
    w = nh * DA_ROW
    lam_spec = _resident((1, DA_HEAD_DIM))
    stat = pltpu.VMEM((nh, 1, 2 * tq), F32)
    once = pl.Buffered(1)
    return pl.pallas_call(
        functools.partial(_attn_prompt_kernel, tq=tq, nh=nh, lam_init=lam_init),
        grid=(B, DA_HEADS // nh, S // tq),
        in_specs=[pl.BlockSpec((None, tq, w), lambda b, h, i: (b, i, h)),
                  pl.BlockSpec((None, S, w), lambda b, h, i: (b, 0, h)),
                  pl.BlockSpec((None, n_vt, w, tv), lambda b, h, i: (b, 0, h, 0),
                               pipeline_mode=once),
                  lam_spec, lam_spec, lam_spec, lam_spec, _resident((DA_ROW, 1))],
        out_specs=pl.BlockSpec((None, tq, w), lambda b, h, i: (b, i, h)),
        out_shape=jax.ShapeDtypeStruct((B, S, D_MODEL), BF16),
        scratch_shapes=[pltpu.VMEM((nh, DA_ROW, 2 * tq), BF16),
                        pltpu.VMEM((nh, tq, 2 * tq), F32), pltpu.VMEM((nh, tq, 2 * tq), F32),
                        stat, stat, pltpu.VMEM((nh, DA_ROW, 2 * tq), F32)],
        compiler_params=_params(3),
        name="attn_prompt",
    )(q, k, vt, *lam_p, nw.reshape(DA_ROW, 1))


_SAMPLE_KV_ROWS = 2048


def _attn_sample_kernel(q_ref, kn_ref, vn_ref, ck_ref, cv_ref, lq1, lk1, lq2, lk2, nw_ref, o_ref,
                        ckb_ref, cvb_ref, *, lam_init):
    L, P = q_ref.shape[0], ck_ref.shape[0]
    n_past = P * DA_HEADS
    ch = min(_SAMPLE_KV_ROWS, n_past)
    rows = 2 * L * DA_HEADS
    flat = lambda r: r[...].reshape(r.shape[0] * DA_HEADS, DA_ROW).astype(BF16)
    ckb_ref[...] = flat(ck_ref)
    cvb_ref[...] = flat(cv_ref)
    kn, vn = flat(kn_ref), flat(vn_ref)
    q_all = jnp.concatenate([_split_q(q_ref[:, h * DA_ROW:(h + 1) * DA_ROW])
                             for h in range(DA_HEADS)], axis=0)

    def own_head(n_cols):
        row_head = lax.broadcasted_iota(jnp.int32, (rows, n_cols), 0) // (2 * L)
        col_head = lax.broadcasted_iota(jnp.int32, (rows, n_cols), 1) % DA_HEADS
        return row_head == col_head

    def update(carry, kc, vc, visible):
        m, l, acc = carry
        s = lax.dot_general(q_all, kc, _NT, preferred_element_type=F32)
        s = jnp.where(visible, s, -jnp.inf)
        m_new = jnp.maximum(m, jnp.max(s, axis=1, keepdims=True))
        alpha = jnp.exp2(m - m_new)
        p = jnp.exp2(s - m_new)
        return (m_new, alpha * l + jnp.sum(p, axis=1, keepdims=True),
                alpha * acc + jnp.dot(p.astype(BF16), vc, preferred_element_type=F32))

    vis_p = own_head(ch)

    def chunk(c, carry):
        r = pl.multiple_of(c * ch, ch)
        return update(carry, ckb_ref[pl.ds(r, ch), :], cvb_ref[pl.ds(r, ch), :], vis_p)

    init = (jnp.full((rows, 1), -jnp.inf, F32), jnp.zeros((rows, 1), F32),
            jnp.zeros((rows, DA_ROW), F32))
    carry = lax.fori_loop(0, n_past // ch, chunk, init)
    _, l, acc = update(carry, kn, vn, own_head(L * DA_HEADS))
    o12 = acc / l
    lam = _lambda(lq1, lk1, lq2, lk2, lam_init)
    for h in range(DA_HEADS):
        o_ref[:, h * DA_ROW:(h + 1) * DA_ROW] = _attn_finish(
            o12[h * 2 * L:(h + 1) * 2 * L], lam, nw_ref[...], lam_init, L)


def _attn_sample(q, k_stack, v_stack, cache_k, cache_v, li, lam_p, nw, lam_init):
    B, L, _ = q.shape
    P = cache_k.shape[2]
    row = pl.BlockSpec((None, L, D_MODEL), lambda b: (b, 0, 0))
    new = pl.BlockSpec((None, L, DA_HEADS, DA_ROW), lambda b: (li, b, 0, 0))
    past = pl.BlockSpec((None, None, P, DA_HEADS, DA_ROW), lambda b: (li, b, 0, 0, 0))
    lam_spec = _resident((1, DA_HEAD_DIM))
    return pl.pallas_call(
        functools.partial(_attn_sample_kernel, lam_init=lam_init),
        grid=(B,),
        in_specs=[row, new, new, past, past, lam_spec, lam_spec, lam_spec, lam_spec,
                  _resident((1, DA_ROW))],
        out_specs=row,
        out_shape=jax.ShapeDtypeStruct((B, L, D_MODEL), BF16),
        scratch_shapes=[pltpu.VMEM((P * DA_HEADS, DA_ROW), BF16)] * 2,
        compiler_params=_params(1),
        name="attn_sample",
    )(q, k_stack, v_stack, cache_k, cache_v, *lam_p, nw)


def _gla_kernel(q_ref, k_ref, v_ref, g_ref, r_ref, s0_ref, nw_ref, o_ref, sout_ref,
                qs_ref, ks_ref, vs_ref, os_ref, S_ref, *, C, nchunk, independent):
    A = C // SUBLANES
    i = pl.program_id(2)

    if not independent:
        @pl.when(i == 0)
        def _():
            S_ref[...] = s0_ref[0]

    qs_ref[...] = q_ref[...].astype(F32)
    ks_ref[...] = k_ref[...].astype(F32)
    for half in range(GLA_DV // LANES):
        vs_ref[half] = v_ref[:, half * LANES:(half + 1) * LANES].astype(F32)

    def time_of(r):
        return (r % SUBLANES) * A + r // SUBLANES

    t_r = time_of(lax.broadcasted_iota(jnp.int32, (C, C), 0))
    t_c = time_of(lax.broadcasted_iota(jnp.int32, (C, C), 1))
    t_k = time_of(lax.broadcasted_iota(jnp.int32, (C, GLA_DK), 0))
    halves = []
    hsz = C // 2
    while hsz >= A:
        halves.append(hsz)
        hsz //= 2
    sel = [t_c <= t_r] + [t_c <= (t_r // (2 * h)) * (2 * h) + h - 1 for h in halves]
    sel = jnp.concatenate([jnp.where(m, 1.0, 0.0) for m in sel], axis=0).astype(BF16)
    same_parent = [(t_r // (2 * h)) == (t_c // (2 * h)) for h in halves]
    second_half = [(t_k // h) % 2 == 1 for h in halves]
    r8 = lax.broadcasted_iota(jnp.int32, (SUBLANES, C), 0)
    c8 = lax.broadcasted_iota(jnp.int32, (SUBLANES, C), 1)
    ones_cv = jnp.ones((C, GLA_DV), BF16)

    def load(ref, base):
        return jnp.concatenate(
            [ref[pl.ds(base + a, SUBLANES, stride=A), :] for a in range(A)], axis=0)

    chunks = range(nchunk)
    q = [load(qs_ref, c * C) for c in chunks]
    k = [load(ks_ref, c * C) for c in chunks]
    g = [load(g_ref, c * C) for c in chunks]
    v = [jnp.concatenate([load(vs_ref.at[half], c * C) for half in range(GLA_DV // LANES)],
                         axis=1).astype(BF16) for c in chunks]
    g_hi = [x.astype(BF16) for x in g]
    g_lo = [(x - hi.astype(F32)).astype(BF16) for x, hi in zip(g, g_hi)]
    sums = [jnp.dot(sel, jnp.concatenate([hi, lo], axis=1), preferred_element_type=F32)
            for hi, lo in zip(g_hi, g_lo)]
    sums = [x[:, :GLA_DK] + x[:, GLA_DK:] for x in sums]
    b = [x[:C] for x in sums]
    decay = [lax.dot_general(hi, ones_cv, _TN, preferred_element_type=F32)
             + lax.dot_general(lo, ones_cv, _TN, preferred_element_type=F32)
             for hi, lo in zip(g_hi, g_lo)]

    amat = []
    for c in chunks:
        am = jnp.zeros((C, C), F32)
        for lvl, h in enumerate(halves):
            x = b[c] - sums[c][(lvl + 1) * C:(lvl + 2) * C]
            sec = second_half[lvl]
            qh = jnp.where(sec, q[c] * jnp.exp(jnp.where(sec, x, 0.0)), 0.0).astype(BF16)
            kh = jnp.where(sec, 0.0, k[c] * jnp.exp(jnp.where(sec, 0.0, -x))).astype(BF16)
            al = lax.dot_general(qh, kh, _NT, preferred_element_type=F32)
            am = am + jnp.where(same_parent[lvl], al, 0.0)
        amat.append(am)
    for c in chunks:
        diag = []
        for a in range(A):
            ra = slice(a * SUBLANES, (a + 1) * SUBLANES)
            acc = jnp.zeros((SUBLANES, C), F32)
            for a2 in range(a + 1):
                rb = slice(a2 * SUBLANES, (a2 + 1) * SUBLANES)
                e = q[c][ra] * k[c][rb] * jnp.exp(b[c][ra] - b[c][rb])
                acc = acc + jnp.where(c8 == a2 * SUBLANES + r8,
                                      jnp.sum(e, axis=1, keepdims=True), 0.0)
            diag.append(acc)
        amat[c] = amat[c] + jnp.concatenate(diag, axis=0)
    o_intra = [jnp.dot(amat[c].astype(BF16), v[c], preferred_element_type=F32) for c in chunks]
    upd = [lax.dot_general((k[c] * jnp.exp(b[c][C - 1:C] - b[c])).astype(BF16), v[c], _TN,
                           preferred_element_type=F32) for c in chunks]
    q_dec = [(q[c] * jnp.exp(b[c])).astype(BF16) for c in chunks]

    S = None if independent else S_ref[...]
    for c in chunks:
        if independent:
            S = s0_ref[c]
        o = o_intra[c] + jnp.dot(q_dec[c], S.astype(BF16), preferred_element_type=F32)
        S = jnp.exp(decay[c]) * S + upd[c]
        if independent:
            sout_ref[c] = S
        for a in range(A):
            for half in range(GLA_DV // LANES):
                os_ref[half, pl.ds(c * C + a, SUBLANES, stride=A), :] = (
                    o[a * SUBLANES:(a + 1) * SUBLANES, half * LANES:(half + 1) * LANES])

    og = jnp.concatenate([os_ref[half] for half in range(GLA_DV // LANES)], axis=1)
    y = _rms(og, nw_ref[...])
    r = r_ref[...].astype(F32)
    o_ref[...] = (y * (r * jax.nn.sigmoid(r))).astype(BF16)

    if not independent:
        S_ref[...] = S

        @pl.when(i == pl.num_programs(2) - 1)
        def _():
            sout_ref[0] = S


_GLA_ROWS = 512
_GLA_ROWS_SHORT = 128


def _gla(qg, kg, vg, gk, rg, s0, nw):
    B, L, _ = qg.shape
    if L % CHUNK == 0:
        C, tc, nb = CHUNK, min(_GLA_ROWS, L), 1
    else:
        C = L
        nb = max(d for d in range(1, B + 1) if B % d == 0 and d * L <= _GLA_ROWS_SHORT)
        tc = nb * L
    G = B // nb
    rows = lambda t: t.reshape(G, nb * L, t.shape[-1])
    dk = lambda b, h, i: (b, i, h)
    st = pl.BlockSpec((None, nb, None, GLA_DK, GLA_DV), lambda b, h, i: (b, 0, h, 0, 0))
    ob, s_new = pl.pallas_call(
        functools.partial(_gla_kernel, C=C, nchunk=tc // C, independent=nb * L == tc and C == L),
        grid=(G, GLA_HEADS, nb * L // tc),
        in_specs=[pl.BlockSpec((None, tc, GLA_DK), dk), pl.BlockSpec((None, tc, GLA_DK), dk),
                  pl.BlockSpec((None, tc, GLA_DV), dk), pl.BlockSpec((None, tc, GLA_DK), dk),
                  pl.BlockSpec((None, tc, GLA_DV), dk), st, _resident((1, GLA_DV))],
        out_specs=[pl.BlockSpec((None, tc, GLA_DV), dk), st],
        out_shape=[jax.ShapeDtypeStruct((G, nb * L, D_MODEL), BF16),
                   jax.ShapeDtypeStruct((G, nb, GLA_HEADS, GLA_DK, GLA_DV), F32)],
        scratch_shapes=[pltpu.VMEM((tc, GLA_DK), F32), pltpu.VMEM((tc, GLA_DK), F32),
                        pltpu.VMEM((GLA_DV // LANES, tc, LANES), F32),
                        pltpu.VMEM((GLA_DV // LANES, tc, LANES), F32),
                        pltpu.VMEM((GLA_DK, GLA_DV), F32)],
        compiler_params=_params(3),
        name="gla",
    )(rows(qg), rows(kg), rows(vg), rows(gk), rows(rg),
      s0.reshape(G, nb, GLA_HEADS, GLA_DK, GLA_DV), nw)
    return ob.reshape(B, L, D_MODEL), s_new.reshape(B, GLA_HEADS, GLA_DK, GLA_DV)


_FF_BLK = 256
_MIX_ROWS = 512
_MIX_ROWS_SHORT = 256


def _mix_kernel(oa_ref, ob_ref, ga_ref, gb_ref, x_ref, wo_ref, nw0_ref,
                nw1_ref, wup_ref, cw_ref, cb_ref, wdn_ref, past_ref, nw2_ref,
                out_ref, cs_ref, act_ref, carry_ref, *, tiles_per_seq, seq_len):
    i = pl.program_id(0)
    tm = x_ref.shape[0]
    whole = tiles_per_seq == 0

    if not whole:
        @pl.when(i % tiles_per_seq == 0)
        def _():
            carry_ref[SUBLANES - 2:SUBLANES, :] = past_ref[...]

    f = lambda r: r[...].astype(F32)
    merged = jax.nn.sigmoid(f(ga_ref)) * f(oa_ref) + jax.nn.sigmoid(f(gb_ref)) * f(ob_ref)
    y = jnp.dot(merged.astype(BF16), wo_ref[...], preferred_element_type=F32)
    h = x_ref[...] + _rms(y, nw0_ref[...])
    hb = _rms(h, nw1_ref[...]).astype(BF16)
    rowid = lax.broadcasted_iota(jnp.int32, (tm, _FF_BLK), 0)
    for blk in range(D_FF // _FF_BLK):
        cols = slice(blk * _FF_BLK, (blk + 1) * _FF_BLK)
        u = jnp.dot(hb, wup_ref[:, cols], preferred_element_type=F32)
        g = jnp.dot(hb, wup_ref[:, D_FF + blk * _FF_BLK:D_FF + (blk + 1) * _FF_BLK],
                    preferred_element_type=F32)
        if whole:
            pos = rowid % seq_len
            prev2, prev1 = past_ref[0, :, cols], past_ref[1, :, cols]
        else:
            pos = rowid
            prev2 = carry_ref[SUBLANES - 2:SUBLANES - 1, cols]
            prev1 = carry_ref[SUBLANES - 1:SUBLANES, cols]
        g1 = jnp.where(pos == 0, prev1, pltpu.roll(g, 1, 0))
        g2 = jnp.where(pos == 0, prev2, jnp.where(pos == 1, prev1, pltpu.roll(g, 2, 0)))
        gc = cb_ref[:, cols] + cw_ref[0:1, cols] * g2 + cw_ref[1:2, cols] * g1 + cw_ref[2:3, cols] * g
        act_ref[:, cols] = (jax.nn.gelu(gc, approximate=True) * u).astype(BF16)
        if whole:
            last = g.reshape(tm // seq_len, seq_len, _FF_BLK)[:, seq_len - (CONV_W - 1):, :]
            cs_ref[:, :, cols] = last
        else:
            carry_ref[:, cols] = g[tm - SUBLANES:tm, :]
            cs_ref[:, cols] = g[tm - (CONV_W - 1):tm, :]
    ffn = jnp.dot(act_ref[...], wdn_ref[...], preferred_element_type=F32)
    out_ref[...] = h + _rms(ffn, nw2_ref[...])


def _mix(oa, ob, ga, gb, x2d, seq_len, wo, nw0, nw1, wup, cw, cb, wdn, conv_past, nw2):
    T = x2d.shape[0]
    nseq = T // seq_len
    if seq_len >= _MIX_ROWS:
        tm = _MIX_ROWS
        tps = seq_len // tm
        past = conv_past
        past_spec = st = pl.BlockSpec((None, CONV_W - 1, D_FF), lambda i: (i // tps, 0, 0))
    else:
        tm = min(_MIX_ROWS_SHORT, T)
        tps = 0
        past = jnp.repeat(jnp.swapaxes(conv_past, 0, 1), seq_len, axis=1)
        past_spec = pl.BlockSpec((CONV_W - 1, tm, D_FF), lambda i: (0, i, 0))
        st = pl.BlockSpec((tm // seq_len, CONV_W - 1, D_FF), lambda i: (i, 0, 0))
    row = pl.BlockSpec((tm, D_MODEL), lambda i: (i, 0))
    vec = _resident((1, D_MODEL))
    return pl.pallas_call(
        functools.partial(_mix_kernel, tiles_per_seq=tps, seq_len=seq_len),
        grid=(T // tm,),
        in_specs=[row, row, row, row, row, _resident((D_MODEL, D_MODEL)), vec,
                  vec, _resident((D_MODEL, 2 * D_FF)),
                  _resident((CONV_W, D_FF)), _resident((1, D_FF)), _resident((D_FF, D_MODEL)),
                  past_spec, vec],
        out_specs=[row, st],
        out_shape=[jax.ShapeDtypeStruct((T, D_MODEL), F32),
                   jax.ShapeDtypeStruct((nseq, CONV_W - 1, D_FF), F32)],
        scratch_shapes=[pltpu.VMEM((tm, D_FF), BF16), pltpu.VMEM((SUBLANES, D_FF), F32)],
        compiler_params=_params(1),
        name="mix",
    )(oa, ob, ga, gb, x2d, wo, nw0, nw1, wup, cw, cb, wdn, past, nw2)


def _layer(x2d, nseq, seq_len, pos0, caches, li, depth, kv_stack, s0, conv_past, p):
    (q, k, v, vt, kb, qg, kg, vg, rg, gk, ga, gb) = _inproj(
        x2d, seq_len, pos0, p["pre_mix_w"], p["w_cat"], p["w_gk2"], p["b_gk2"], li, depth,
        kv_stack)
    lam_init = 0.8 - 0.6 * math.exp(-0.3 * li)
    lam_p = (p["lq1"], p["lk1"], p["lq2"], p["lk2"])
    sq = lambda t: t.reshape(nseq, seq_len, t.shape[-1])
    if caches is None:
        vt = vt.reshape(nseq, vt.shape[0] // nseq, D_MODEL, vt.shape[-1])
        oa = _attn_prompt(sq(q), sq(kb), vt, lam_p, p["da_norm_w"], lam_init)
    else:
        oa = _attn_sample(sq(q), k, v, caches[0], caches[1], li, lam_p, p["da_norm_w"], lam_init)
    ob, s_new = _gla(sq(qg), sq(kg), sq(vg), sq(gk), sq(rg), s0, p["gla_norm_w"])
    T = x2d.shape[0]
    out, conv_new = _mix(oa.reshape(T, D_MODEL), ob.reshape(T, D_MODEL), ga, gb, x2d, seq_len,
                         p["w_o"], p["post_mix_w"], p["pre_ffn_w"], p["w_up"], p["conv_w"],
                         p["conv_b"], p["w_down"], conv_past, p["post_ffn_w"])
    return out, (k, v), s_new, conv_new


def kernel(x_prompt, x_sample, cache_k, cache_v, state_gla, state_conv, w_in, w_gk2, b_gk2, lambda_q1, lambda_k1, lambda_q2, lambda_k2, da_norm_w, gla_norm_w, w_o, pre_mix_w, post_mix_w, pre_ffn_w, post_ffn_w, w_up, conv_w, conv_b, w_down):
    B, S, _ = x_prompt.shape
    Bs, L, _ = x_sample.shape
    depth, _, P = cache_k.shape[:3]
    hp = x_prompt.reshape(B * S, D_MODEL)
    hs = x_sample.reshape(Bs * L, D_MODEL)
    s0_p = jnp.zeros((B, GLA_HEADS, GLA_DK, GLA_DV), F32)
    conv0_p = jnp.zeros((B, CONV_W - 1, D_FF), F32)
    c_lr = _C_LR - 2 * D_MODEL + GLA_RANK
    outs_p, outs_s = [], []
    kv_p = kv_s = None
    for li in range(depth):
        w = w_in[li]
        w_cat = jnp.concatenate(
            [w[:, :c_lr - GLA_RANK], w[:, c_lr:], w[:, c_lr - GLA_RANK:c_lr],
             jnp.zeros((D_MODEL, LANES - GLA_RANK), w.dtype)], axis=1).astype(BF16)
        p = dict(
            w_cat=w_cat,
            w_gk2=jnp.concatenate([w_gk2[li], jnp.zeros((LANES - GLA_RANK, w_gk2.shape[-1]),
                                                        w_gk2.dtype)], axis=0).astype(BF16),
            b_gk2=b_gk2[li][None, :],
            lq1=lambda_q1[li][None, :], lk1=lambda_k1[li][None, :],
            lq2=lambda_q2[li][None, :], lk2=lambda_k2[li][None, :],
            da_norm_w=da_norm_w[li][None, :], gla_norm_w=gla_norm_w[li][None, :],
            w_o=w_o[li].astype(BF16),
            pre_mix_w=pre_mix_w[li][None, :], post_mix_w=post_mix_w[li][None, :],
            pre_ffn_w=pre_ffn_w[li][None, :], post_ffn_w=post_ffn_w[li][None, :],
            w_up=w_up[li].astype(BF16), conv_w=conv_w[li], conv_b=conv_b[li][None, :],
            w_down=w_down[li].astype(BF16))
        hp, kv_p, *rest_p = _layer(hp, B, S, 0, None, li, depth, kv_p, s0_p, conv0_p, p)
        hs, kv_s, *rest_s = _layer(hs, Bs, L, P, (cache_k, cache_v), li, depth, kv_s,
                                   state_gla[li], state_conv[li], p)
        outs_p.append(rest_p)
        outs_s.append(rest_s)
    stack = lambda outs, j: jnp.stack([o[j] for o in outs])
    kv5 = lambda t, n, l: t.reshape(depth, n, l, DA_HEADS, DA_ROW)
    return (hp.reshape(B, S, D_MODEL), hs.reshape(Bs, L, D_MODEL),
            kv5(kv_p[0], B, S), kv5(kv_p[1], B, S), stack(outs_p, 0), stack(outs_p, 1),
            kv5(kv_s[0], Bs, L), kv5(kv_s[1], Bs, L), stack(outs_s, 0), stack(outs_s, 1))
```
